```python
import math
import jax, jax.numpy as jnp
from jax import lax
import numpy as np

D_MODEL = 1024
BATCH = 16
SEQ = 4096
DEPTH = 4

N_MIXERS = 3
CONV_WIDTH = 31
POOL_WINDOWS = (2, 4, 8, 16)
N_POOL_GROUPS = len(POOL_WINDOWS)
POOL_GROUP_DIM = D_MODEL // N_POOL_GROUPS
N_HEADS = 16
HEAD_DIM = D_MODEL // N_HEADS
Q_BLOCK = 128
D_FF = ((8 * D_MODEL // 3 + 127) // 128) * 128
FFN_CONV_WIDTH = 3
EPS = 1e-6
N_A = (DEPTH + 2) // 3
N_B = (DEPTH + 1) // 3
N_C = DEPTH // 3

kernel_name = "hybrid_conv_pool_fox_trunk"


def rms_norm(x, g):
    x32 = x.astype(jnp.float32)
    y = x32 * lax.rsqrt(jnp.mean(x32 * x32, axis=-1, keepdims=True) + EPS)
    return (y * g.astype(jnp.float32)).astype(x.dtype)


def causal_dwconv(x, w, b):
    k_width, c = w.shape
    y = lax.conv_general_dilated(x, w[:, None, :].astype(x.dtype), window_strides=(1,),
                                 padding=[(k_width - 1, 0)],
                                 dimension_numbers=("NWC", "WIO", "NWC"),
                                 feature_group_count=c)
    return y + b.astype(x.dtype)


def conformer_conv(h, w_in, b_in, dw, dw_b, ln_g, ln_b, w_out, b_out):
    a, g = jnp.split(h @ w_in + b_in, 2, axis=-1)
    u = causal_dwconv(a * jax.nn.sigmoid(g), dw, dw_b)
    u32 = u.astype(jnp.float32)
    mu = jnp.mean(u32, axis=-1, keepdims=True)
    var = jnp.mean(jnp.square(u32 - mu), axis=-1, keepdims=True)
    u = ((u32 - mu) * lax.rsqrt(var + EPS) * ln_g.astype(jnp.float32) + ln_b.astype(jnp.float32)).astype(h.dtype)
    return jax.nn.silu(u) @ w_out + b_out


def multiscale_pool(h, w_grp, b_grp, scale):
    bsz, s, d = h.shape
    hg = h.reshape(bsz, s, N_POOL_GROUPS, POOL_GROUP_DIM).astype(jnp.float32)
    t = jnp.arange(s)
    outs = []
    for gi, w in enumerate(POOL_WINDOWS):
        xg = hg[:, :, gi]
        cs = jnp.cumsum(xg, axis=1)
        lag = jnp.pad(cs[:, :s - w], ((0, 0), (w, 0), (0, 0)))
        cnt = jnp.minimum(t + 1, w).astype(jnp.float32)[None, :, None]
        outs.append((cs - lag) / cnt - xg)
    p = jnp.stack(outs, axis=2).astype(h.dtype)
    y = jnp.einsum("bsgc,gcd->bsgd", p, w_grp) + b_grp
    return y.reshape(bsz, s, d) * scale


def fox_block_attention(q, k, v, c):
    bsz, nh, s, hd = q.shape
    nb = s // Q_BLOCK
    scale = 1.0 / math.sqrt(hd)
    qb = q.reshape(bsz, nh, nb, Q_BLOCK, hd).transpose(2, 0, 1, 3, 4)
    cb = c.reshape(bsz, nh, nb, Q_BLOCK).transpose(2, 0, 1, 3)
    pos = jnp.arange(s)
    qpos = pos.reshape(nb, Q_BLOCK)

    def one_block(args):
        qi, ci, pi = args
        logits = jnp.einsum("bhqd,bhkd->bhqk", qi, k).astype(jnp.float32) * scale
        logits = logits + ci[..., :, None] - c[:, :, None, :]
        mask = pi[:, None] >= pos[None, :]
        logits = jnp.where(mask[None, None], logits, -jnp.inf)
        probs = jax.nn.softmax(logits, axis=-1)
        return jnp.einsum("bhqk,bhkd->bhqd", probs.astype(v.dtype), v)

    out = lax.map(one_block, (qb, cb, qpos))
    return out.transpose(1, 2, 0, 3, 4).reshape(bsz, nh, s, hd)


def forgetting_attention(h, w_in, b_f, q_gain, k_gain, w_o):
    bsz, s, d = h.shape
    proj = h @ w_in
    q, k, v, fl = jnp.split(proj, [d, 2 * d, 3 * d], axis=-1)
    to_heads = lambda z: z.reshape(bsz, s, N_HEADS, HEAD_DIM).transpose(0, 2, 1, 3)
    q = rms_norm(to_heads(q), q_gain)
    k = rms_norm(to_heads(k), k_gain)
    v = to_heads(v)
    logf = jax.nn.log_sigmoid(fl.astype(jnp.float32) + b_f.astype(jnp.float32))
    c = jnp.cumsum(logf, axis=1).transpose(0, 2, 1)
    o = fox_block_attention(q, k, v, c)
    return o.transpose(0, 2, 1, 3).reshape(bsz, s, d) @ w_o


def conv_ffn(h, w_up, dw, dw_b, w_down):
    u = causal_dwconv(h @ w_up, dw, dw_b)
    val, gate = jnp.split(u, 2, axis=-1)
    return (jax.nn.silu(gate) * val) @ w_down


def setup_inputs(seed: int = 0) -> dict:
    key = jax.random.key(seed)
    ks = iter(jax.random.split(key, 32))
    nrm = lambda shape, s: jax.random.normal(next(ks), shape, jnp.float32) * s
    D, F = D_MODEL, D_FF
    return {
        "x": nrm((BATCH, SEQ, D), 1.0),
        "norm_mix": 1.0 + nrm((DEPTH, D), 0.02),
        "norm_ffn": 1.0 + nrm((DEPTH, D), 0.02),
        "conv_w_in": nrm((N_A, D, 2 * D), D ** -0.5),
        "conv_b_in": nrm((N_A, 2 * D), 0.02),
        "conv_dw": nrm((N_A, CONV_WIDTH, D), CONV_WIDTH ** -0.5),
        "conv_dw_b": nrm((N_A, D), 0.02),
        "conv_ln_g": 1.0 + nrm((N_A, D), 0.02),
        "conv_ln_b": nrm((N_A, D), 0.02),
        "conv_w_out": nrm((N_A, D, D), D ** -0.5),
        "conv_b_out": nrm((N_A, D), 0.02),
        "pool_w": nrm((N_B, N_POOL_GROUPS, POOL_GROUP_DIM, POOL_GROUP_DIM), POOL_GROUP_DIM ** -0.5),
        "pool_b": nrm((N_B, N_POOL_GROUPS, POOL_GROUP_DIM), 0.02),
        "pool_scale": 0.5 + nrm((N_B, D), 0.05),
        "fox_w_in": nrm((N_C, D, 3 * D + N_HEADS), D ** -0.5),
        "fox_b_f": 2.0 + nrm((N_C, N_HEADS), 0.5),
        "fox_q_gain": 1.0 + nrm((N_C, HEAD_DIM), 0.02),
        "fox_k_gain": 1.0 + nrm((N_C, HEAD_DIM), 0.02),
        "fox_w_o": nrm((N_C, D, D), D ** -0.5),
        "ffn_w_up": nrm((DEPTH, D, 2 * F), D ** -0.5),
        "ffn_dw": nrm((DEPTH, FFN_CONV_WIDTH, 2 * F), FFN_CONV_WIDTH ** -0.5),
        "ffn_dw_b": nrm((DEPTH, 2 * F), 0.02),
        "ffn_w_down": nrm((DEPTH, F, D), F ** -0.5),
    }


def reference(x, norm_mix, norm_ffn, conv_w_in, conv_b_in, conv_dw, conv_dw_b, conv_ln_g, conv_ln_b,
              conv_w_out, conv_b_out, pool_w, pool_b, pool_scale, fox_w_in, fox_b_f, fox_q_gain,
              fox_k_gain, fox_w_o, ffn_w_up, ffn_dw, ffn_dw_b, ffn_w_down):
    for i in range(DEPTH):
        j = i // N_MIXERS
        h = rms_norm(x, norm_mix[i])
        kind = i % N_MIXERS
        if kind == 0:
            y = conformer_conv(h, conv_w_in[j], conv_b_in[j], conv_dw[j], conv_dw_b[j],
                               conv_ln_g[j], conv_ln_b[j], conv_w_out[j], conv_b_out[j])
        elif kind == 1:
            y = multiscale_pool(h, pool_w[j], pool_b[j], pool_scale[j])
        else:
            y = forgetting_attention(h, fox_w_in[j], fox_b_f[j], fox_q_gain[j], fox_k_gain[j], fox_w_o[j])
        x = x + y
        x = x + conv_ffn(rms_norm(x, norm_ffn[i]), ffn_w_up[i], ffn_dw[i], ffn_dw_b[i], ffn_w_down[i])
    return x
```

```python
import functools
import math

import jax
import jax.numpy as jnp
from jax import lax
from jax.experimental import pallas as pl
from jax.experimental.pallas import tpu as pltpu

F32 = jnp.float32
BF16 = jnp.bfloat16

EPS = 1e-6
N_HEADS = 16
HEAD_DIM = 64
CONV_WIDTH = 31
FFN_CONV_WIDTH = 3
POOL_WINDOWS = (2, 4, 8, 16)
LANES = 128
SUBLANES = 8
MASK_VALUE = -1e30
BIAS_ROWS = 16

TOKEN_TILE = 512
ATTN_TILE = 512
FFN_CHUNK = 256
CONV_ROWS = 64
VMEM_LIMIT = 48 * 1024 * 1024


def _const_spec(shape):
    nd = len(shape)
    return pl.BlockSpec(shape, lambda *_: (0,) * nd, pipeline_mode=pl.Buffered(1))


def _params():
    return pltpu.CompilerParams(dimension_semantics=("arbitrary",), vmem_limit_bytes=VMEM_LIMIT)


def _rms(x, g):
    ms = jnp.mean(x * x, axis=-1, keepdims=True)
    return x * lax.rsqrt(ms + EPS) * g


def _silu(x):
    return x * jax.nn.sigmoid(x)


def _ffn_kernel(x_ref, g_ref, wup_ref, dwp_ref, wdn_ref, o_ref, hb_ref, ubuf_ref, carry_ref, *, tm, nc,
                tiles_per_seq):
    i = pl.program_id(0)
    x = x_ref[...]
    hb_ref[...] = _rms(x, g_ref[...]).astype(BF16)

    @pl.when(i % tiles_per_seq == 0)
    def _():
        carry_ref[...] = jnp.zeros_like(carry_ref)

    o_ref[...] = x

    def chunk(c, _):
        ys = []
        for p in range(2):
            u = jnp.dot(hb_ref[...], wup_ref[p, c], preferred_element_type=F32)
            ubuf_ref[p, 0:SUBLANES, :] = carry_ref[p, c]
            ubuf_ref[p, SUBLANES:SUBLANES + tm, :] = u
            carry_ref[p, c] = u[tm - SUBLANES:tm, :]
            t = dwp_ref[p, c]
            y = (t[2:3] * u + t[1:2] * ubuf_ref[p, SUBLANES - 1:SUBLANES - 1 + tm, :]
                 + t[0:1] * ubuf_ref[p, SUBLANES - 2:SUBLANES - 2 + tm, :]) + t[3:4]
            ys.append(y)
        val, gate = ys
        a = (_silu(gate) * val).astype(BF16)
        o_ref[...] += jnp.dot(a, wdn_ref[c], preferred_element_type=F32)
        return 0

    lax.fori_loop(0, nc, chunk, 0)


def _ffn(x2, seq, g, w_up, dw, dw_b, w_down):
    n, d = x2.shape
    f = w_down.shape[0]
    fc = FFN_CHUNK
    nc = f // fc
    tm = min(TOKEN_TILE, seq)
    assert f % fc == 0 and seq % tm == 0 and dw.shape[0] == FFN_CONV_WIDTH
    wup = w_up.astype(BF16).reshape(d, 2, nc, fc).transpose(1, 2, 0, 3)
    taps = jnp.concatenate([dw, dw_b[None, :], jnp.zeros((SUBLANES - 4, 2 * f), F32)], axis=0)
    dwp = taps.reshape(SUBLANES, 2, nc, fc).transpose(1, 2, 0, 3)
    wdn = w_down.astype(BF16).reshape(nc, fc, d)
    kern = functools.partial(_ffn_kernel, tm=tm, nc=nc, tiles_per_seq=seq // tm)
    return pl.pallas_call(
        kern,
        out_shape=jax.ShapeDtypeStruct((n, d), F32),
        grid=(n // tm,),
        in_specs=[
            pl.BlockSpec((tm, d), lambda i: (i, 0)),
            _const_spec((1, d)),
            _const_spec((2, nc, d, fc)),
            _const_spec((2, nc, SUBLANES, fc)),
            _const_spec((nc, fc, d)),
        ],
        out_specs=pl.BlockSpec((tm, d), lambda i: (i, 0)),
        scratch_shapes=[
            pltpu.VMEM((tm, d), BF16),
            pltpu.VMEM((2, SUBLANES + tm, fc), F32),
            pltpu.VMEM((2, nc, SUBLANES, fc), F32),
        ],
        compiler_params=_params(),
        name="conv_ffn",
    )(x2, g[None, :], wup, dwp, wdn)


def _conf_kernel(x_ref, g_ref, win_ref, bin_ref, dws_ref, lng_ref, lnb_ref, wout_ref, bout_ref, o_ref, vbuf_ref,
                 ybuf_ref, *, tm, d, tiles_per_seq):
    i = pl.program_id(0)
    halo = 32
    nslab = d // LANES
    x = x_ref[...]
    hb = _rms(x, g_ref[...]).astype(BF16)
    p = jnp.dot(hb, win_ref[...], preferred_element_type=F32) + bin_ref[...]
    v = p[:, :d] * jax.nn.sigmoid(p[:, d:])

    @pl.when(i % tiles_per_seq == 0)
    def _():
        vbuf_ref[:, 0:halo, :] = jnp.zeros((nslab, halo, LANES), F32)

    @pl.when(i % tiles_per_seq != 0)
    def _():
        vbuf_ref[:, 0:halo, :] = vbuf_ref[:, tm:tm + halo, :]

    for j in range(nslab):
        vbuf_ref[j, halo:halo + tm, :] = v[:, j * LANES:(j + 1) * LANES]

    rb = CONV_ROWS
    first = halo - (CONV_WIDTH - 1)

    def slab(j, _):
        def rblock(r, _):
            base = pl.multiple_of(r * rb, rb)
            acc = jnp.broadcast_to(dws_ref[j, CONV_WIDTH:CONV_WIDTH + 1, :], (rb, LANES))
            for k in range(CONV_WIDTH):
                acc = acc + dws_ref[j, k:k + 1, :] * vbuf_ref[j, pl.ds(base + first + k, rb), :]
            ybuf_ref[j, pl.ds(base, rb), :] = acc
            return 0

        lax.fori_loop(0, tm // rb, rblock, 0)
        return 0

    lax.fori_loop(0, nslab, slab, 0)

    y = jnp.concatenate([ybuf_ref[j] for j in range(nslab)], axis=-1)
    mu = jnp.mean(y, axis=-1, keepdims=True)
    yc = y - mu
    var = jnp.mean(yc * yc, axis=-1, keepdims=True)
    yn = yc * lax.rsqrt(var + EPS) * lng_ref[...] + lnb_ref[...]
    sw = _silu(yn).astype(BF16)
    o_ref[...] = x + jnp.dot(sw, wout_ref[...], preferred_element_type=F32) + bout_ref[...]


def _conformer(x2, seq, g, w_in, b_in, dw, dw_b, ln_g, ln_b, w_out, b_out):
    n, d = x2.shape
    tm = min(TOKEN_TILE, seq)
    nslab = d // LANES
    halo = 32
    assert dw.shape[0] == CONV_WIDTH and seq % tm == 0 and tm % CONV_ROWS == 0
    taps = jnp.concatenate([dw, dw_b[None, :]], axis=0)
    dws = taps.reshape(CONV_WIDTH + 1, nslab, LANES).transpose(1, 0, 2)
    kern = functools.partial(_conf_kernel, tm=tm, d=d, tiles_per_seq=seq // tm)
    row = lambda a: a[None, :]
    return pl.pallas_call(
        kern,
        out_shape=jax.ShapeDtypeStruct((n, d), F32),
        grid=(n // tm,),
        in_specs=[
            pl.BlockSpec((tm, d), lambda i: (i, 0)),
            _const_spec((1, d)),
            _const_spec((d, 2 * d)),
            _const_spec((1, 2 * d)),
            _const_spec((nslab, CONV_WIDTH + 1, LANES)),
            _const_spec((1, d)),
            _const_spec((1, d)),
            _const_spec((d, d)),
            _const_spec((1, d)),
        ],
        out_specs=pl.BlockSpec((tm, d), lambda i: (i, 0)),
        scratch_shapes=[
            pltpu.VMEM((nslab, halo + tm, LANES), F32),
            pltpu.VMEM((nslab, tm, LANES), F32),
        ],
        compiler_params=_params(),
        name="conformer_conv",
    )(x2, row(g), w_in.astype(BF16), row(b_in), dws, row(ln_g), row(ln_b), w_out.astype(BF16), row(b_out))


def _pool_kernel(x_ref, g_ref, pw_ref, pb_ref, sc_ref, o_ref, hbuf_ref, *, tm, d, tiles_per_seq):
    i = pl.program_id(0)
    halo = 16
    cg = d // len(POOL_WINDOWS)
    x = x_ref[...]
    h = _rms(x, g_ref[...])

    @pl.when(i % tiles_per_seq == 0)
    def _():
        hbuf_ref[0:halo, :] = jnp.zeros((halo, d), F32)

    @pl.when(i % tiles_per_seq != 0)
    def _():
        hbuf_ref[0:halo, :] = hbuf_ref[tm:tm + halo, :]

    hbuf_ref[halo:halo + tm, :] = h
    pos = (i % tiles_per_seq) * tm + lax.broadcasted_iota(jnp.int32, (tm, cg), 0)
    outs = []
    for gi, w in enumerate(POOL_WINDOWS):
        cols = slice(gi * cg, (gi + 1) * cg)
        hg = h[:, cols]
        s = hg
        for dlt in range(1, w):
            s = s + hbuf_ref[halo - dlt:halo - dlt + tm, cols]
        cnt = jnp.minimum(pos + 1, w).astype(F32)
        pg = (s / cnt - hg).astype(BF16)
        yg = jnp.dot(pg, pw_ref[gi], preferred_element_type=F32) + pb_ref[gi]
        outs.append(yg)
    y = jnp.concatenate(outs, axis=-1)
    o_ref[...] = x + y * sc_ref[...]


def _pool(x2, seq, g, pw, pb, scale):
    n, d = x2.shape
    tm = min(TOKEN_TILE, seq)
    ng = len(POOL_WINDOWS)
    cg = d // ng
    assert seq % tm == 0
    kern = functools.partial(_pool_kernel, tm=tm, d=d, tiles_per_seq=seq // tm)
    return pl.pallas_call(
        kern,
        out_shape=jax.ShapeDtypeStruct((n, d), F32),
        grid=(n // tm,),
        in_specs=[
            pl.BlockSpec((tm, d), lambda i: (i, 0)),
            _const_spec((1, d)),
            _const_spec((ng, cg, cg)),
            _const_spec((ng, 1, cg)),
            _const_spec((1, d)),
        ],
        out_specs=pl.BlockSpec((tm, d), lambda i: (i, 0)),
        scratch_shapes=[pltpu.VMEM((16 + tm, d), F32)],
        compiler_params=_params(),
        name="pool_mixer",
    )(x2, g[None, :], pw.astype(BF16), pb[:, None, :], scale[None, :])


def _fox_proj_kernel(x_ref, g_ref, wq_ref, wkt_ref, wv_ref, wft_ref, bf_ref, e_ref, qg_ref, kg_ref, q_ref, kt_ref,
                     v_ref, c_ref, carry_ref, *, tm, d, tiles_per_seq):
    i = pl.program_id(0)
    hb = _rms(x_ref[...], g_ref[...]).astype(BF16)
    nt = (((1,), (1,)), ((), ()))

    q = jnp.dot(hb, wq_ref[...], preferred_element_type=F32)
    qms = jnp.dot((q * q).astype(BF16), e_ref[...], preferred_element_type=F32)
    q_ref[...] = (q * lax.rsqrt(qms + EPS) * qg_ref[...]).astype(BF16)

    kt = lax.dot_general(wkt_ref[...], hb, nt, preferred_element_type=F32)
    kt3 = kt.reshape(N_HEADS, HEAD_DIM, tm)
    kms = jnp.mean(kt3 * kt3, axis=1, keepdims=True)
    kgain = jnp.tile(kg_ref[...], (1, tm // LANES))
    kt_ref[...] = ((kt3 * lax.rsqrt(kms + EPS)).reshape(d, tm) * kgain).astype(BF16)

    v_ref[...] = jnp.dot(hb, wv_ref[...], preferred_element_type=F32).astype(BF16)

    z = lax.dot_general(wft_ref[...], hb, nt, preferred_element_type=F32) + bf_ref[...]
    logf = jnp.minimum(z, 0.0) - jnp.log1p(jnp.exp(-jnp.abs(z)))
    lane = lax.broadcasted_iota(jnp.int32, logf.shape, 1)
    c = logf
    sh = 1
    while sh < tm:
        c = c + jnp.where(lane >= sh, pltpu.roll(c, sh, axis=1), 0.0)
        sh *= 2

    @pl.when(i % tiles_per_seq == 0)
    def _():
        carry_ref[...] = jnp.zeros_like(carry_ref)

    c = c + jnp.tile(carry_ref[...], (1, tm // LANES))
    c_ref[...] = c
    carry_ref[...] = jnp.broadcast_to(c[:, tm - 1:tm], carry_ref.shape)


def _fox_proj(x2, seq, g, w_in, b_f, q_gain, k_gain):
    n, d = x2.shape
    tm = min(TOKEN_TILE, seq)
    nh = N_HEADS
    assert d == nh * HEAD_DIM and seq % tm == 0
    wq = w_in[:, :d].astype(BF16)
    wkt = w_in[:, d:2 * d].T.astype(BF16)
    wv = w_in[:, 2 * d:3 * d].astype(BF16)
    wft = w_in[:, 3 * d:].T.astype(BF16)
    bfb = jnp.broadcast_to(b_f[:, None], (nh, tm)).astype(F32)
    head = jnp.arange(d) // HEAD_DIM
    e = (head[:, None] == head[None, :]).astype(BF16) * (1.0 / HEAD_DIM)
    qg = (jnp.tile(q_gain, nh) * (1.0 / math.sqrt(HEAD_DIM)))[None, :]
    kg = jnp.broadcast_to(jnp.tile(k_gain, nh)[:, None], (d, LANES)).astype(F32)
    kern = functools.partial(_fox_proj_kernel, tm=tm, d=d, tiles_per_seq=seq // tm)
    return pl.pallas_call(
        kern,
        out_shape=(
            jax.ShapeDtypeStruct((n, d), BF16),
            jax.ShapeDtypeStruct((d, n), BF16),
            jax.ShapeDtypeStruct((n, d), BF16),
            jax.ShapeDtypeStruct((nh, n), F32),
        ),
        grid=(n // tm,),
        in_specs=[
            pl.BlockSpec((tm, d), lambda i: (i, 0)),
            _const_spec((1, d)),
            _const_spec((d, d)),
            _const_spec((d, d)),
            _const_spec((d, d)),
            _const_spec((nh, d)),
            _const_spec((nh, tm)),
            _const_spec((d, d)),
            _const_spec((1, d)),
            _const_spec((d, LANES)),
        ],
        out_specs=(
            pl.BlockSpec((tm, d), lambda i: (i, 0)),
            pl.BlockSpec((d, tm), lambda i: (0, i)),
            pl.BlockSpec((tm, d), lambda i: (i, 0)),
            pl.BlockSpec((nh, tm), lambda i: (0, i)),
        ),
        scratch_shapes=[pltpu.VMEM((nh, LANES), F32)],
        compiler_params=_params(),
        name="fox_proj",
    )(x2, g[None, :], wq, wkt, wv, wft, bfb, e, qg, kg)


def _split3(a):
    hi = a.astype(BF16).astype(F32)
    r = a - hi
    mid = r.astype(BF16).astype(F32)
    lo = (r - mid).astype(BF16).astype(F32)
    return hi, mid, lo


def _attn_kernel(q_ref, kt_ref, v_ref, c_ref, o_ref, krhs_ref, m_ref, l_ref, acc_ref, *, t, nblk):
    hp = pl.program_id(1)
    i = pl.program_id(2)
    kdim = 2 * LANES

    @pl.when(i == 0)
    def _():
        brow = lax.broadcasted_iota(jnp.int32, (BIAS_ROWS, t), 0)
        for e in range(2):
            crow = c_ref[pl.ds(2 * hp + e, 1), :]
            hi, mid, lo = (jnp.broadcast_to(a, (BIAS_ROWS, a.shape[1])) for a in _split3(-crow))
            for jb in range(nblk):
                cols = slice(jb * t, (jb + 1) * t)
                for eo in range(2):
                    rows = slice(eo * HEAD_DIM, (eo + 1) * HEAD_DIM)
                    if eo == e:
                        krhs_ref[e, jb, rows, :] = kt_ref[rows, cols]
                    else:
                        krhs_ref[e, jb, rows, :] = jnp.zeros((HEAD_DIM, t), BF16)
                bias = jnp.where(brow == 0, hi[:, cols], jnp.where(brow == 1, mid[:, cols],
                                 jnp.where(brow == 2, lo[:, cols], 0.0)))
                krhs_ref[e, jb, LANES:LANES + BIAS_ROWS, :] = bias.astype(BF16)
                krhs_ref[e, jb, LANES + BIAS_ROWS:kdim, :] = jnp.zeros((kdim - LANES - BIAS_ROWS, t), BF16)

    q = q_ref[...]
    lhs = jnp.concatenate([q, jnp.ones_like(q)], axis=1)
    m_ref[...] = jnp.full(m_ref.shape, MASK_VALUE, F32)
    l_ref[...] = jnp.zeros(l_ref.shape, F32)
    acc_ref[...] = jnp.zeros(acc_ref.shape, F32)

    def step(j, masked):
        off = pl.multiple_of(j * t, t)
        vblk = v_ref[pl.ds(off, t), :]
        for e in range(2):
            s = jnp.dot(lhs, krhs_ref[e, j], preferred_element_type=F32)
            if masked:
                r = lax.broadcasted_iota(jnp.int32, (t, t), 0)
                cc = lax.broadcasted_iota(jnp.int32, (t, t), 1)
                s = jnp.where(cc <= r, s, MASK_VALUE)
            m_prev = m_ref[e]
            m_new = jnp.maximum(m_prev, jnp.max(s, axis=1, keepdims=True))
            alpha = jnp.exp(m_prev - m_new)
            p = jnp.exp(s - jnp.tile(m_new, (1, t // LANES)))
            l_ref[e] = alpha * l_ref[e] + jnp.sum(p, axis=1, keepdims=True)
            acc_ref[e] = alpha * acc_ref[e] + jnp.dot(p.astype(BF16), vblk, preferred_element_type=F32)
            m_ref[e] = m_new

    def body(j, _):
        step(j, False)
        return 0

    lax.fori_loop(0, i, body, 0)
    step(i, True)

    lane = lax.broadcasted_iota(jnp.int32, (t, LANES), 1)
    o = jnp.where(lane < HEAD_DIM, acc_ref[0] / l_ref[0], acc_ref[1] / l_ref[1])
    o_ref[...] = o.astype(BF16)


def _attention(q, kt, v, c, bsz, seq):
    n, d = q.shape
    t = min(ATTN_TILE, seq)
    nblk = seq // t
    npair = d // LANES
    assert seq % t == 0 and 2 * HEAD_DIM == LANES
    kern = functools.partial(_attn_kernel, t=t, nblk=nblk)
    return pl.pallas_call(
        kern,
        out_shape=jax.ShapeDtypeStruct((n, d), BF16),
        grid=(bsz, npair, nblk),
        in_specs=[
            pl.BlockSpec((t, LANES), lambda b, hp, i: (b * nblk + i, hp)),
            pl.BlockSpec((LANES, seq), lambda b, hp, i: (hp, b)),
            pl.BlockSpec((seq, LANES), lambda b, hp, i: (b, hp)),
            pl.BlockSpec((N_HEADS, seq), lambda b, hp, i: (0, b)),
        ],
        out_specs=pl.BlockSpec((t, LANES), lambda b, hp, i: (b * nblk + i, hp)),
        scratch_shapes=[
            pltpu.VMEM((2, nblk, 2 * LANES, t), BF16),
            pltpu.VMEM((2, t, LANES), F32),
            pltpu.VMEM((2, t, LANES), F32),
            pltpu.VMEM((2, t, LANES), F32),
        ],
        compiler_params=pltpu.CompilerParams(dimension_semantics=("arbitrary",) * 3, vmem_limit_bytes=VMEM_LIMIT),
        name="fox_attention",
    )(q, kt, v, c)


def _out_proj_kernel(x_ref, o_ref, w_ref, y_ref):
    y_ref[...] = x_ref[...] + jnp.dot(o_ref[...], w_ref[...], preferred_element_type=F32)


def _out_proj(x2, o, w_o):
    n, d = x2.shape
    tm = min(TOKEN_TILE, n)
    return pl.pallas_call(
        _out_proj_kernel,
        out_shape=jax.ShapeDtypeStruct((n, d), F32),
        grid=(n // tm,),
        in_specs=[
            pl.BlockSpec((tm, d), lambda i: (i, 0)),
            pl.BlockSpec((tm, d), lambda i: (i, 0)),
            _const_spec((d, d)),
        ],
        out_specs=pl.BlockSpec((tm, d), lambda i: (i, 0)),
        compiler_params=_params(),
        name="fox_out_proj",
    )(x2, o, w_o.astype(BF16))


def _fox(x2, bsz, seq, g, w_in, b_f, q_gain, k_gain, w_o):
    q, kt, v, c = _fox_proj(x2, seq, g, w_in, b_f, q_gain, k_gain)
    o = _attention(q, kt, v, c, bsz, seq)
    return _out_proj(x2, o, w_o)


def kernel(x, norm_mix, norm_ffn, conv_w_in, conv_b_in, conv_dw, conv_dw_b, conv_ln_g, conv_ln_b, conv_w_out,
           conv_b_out, pool_w, pool_b, pool_scale, fox_w_in, fox_b_f, fox_q_gain, fox_k_gain, fox_w_o, ffn_w_up,
           ffn_dw, ffn_dw_b, ffn_w_down):
    bsz, seq, d = x.shape
    depth = norm_mix.shape[0]
    x2 = x.reshape(bsz * seq, d)
    for i in range(depth):
        j = i // 3
        kind = i % 3
        if kind == 0:
            x2 = _conformer(x2, seq, norm_mix[i], conv_w_in[j], conv_b_in[j], conv_dw[j], conv_dw_b[j],
                            conv_ln_g[j], conv_ln_b[j], conv_w_out[j], conv_b_out[j])
        elif kind == 1:
            x2 = _pool(x2, seq, norm_mix[i], pool_w[j], pool_b[j], pool_scale[j])
        else:
            x2 = _fox(x2, bsz, seq, norm_mix[i], fox_w_in[j], fox_b_f[j], fox_q_gain[j], fox_k_gain[j], fox_w_o[j])
        x2 = _ffn(x2, seq, norm_ffn[i], ffn_w_up[i], ffn_dw[i], ffn_dw_b[i], ffn_w_down[i])
    return x2.reshape(bsz, seq, d)
```

```python
import functools
import math

import jax
import jax.numpy as jnp
from jax import lax
from jax.experimental import pallas as pl
from jax.experimental.pallas import tpu as pltpu

F32 = jnp.float32
BF16 = jnp.bfloat16

EPS = 1e-6
N_HEADS = 16
HEAD_DIM = 64
CONV_WIDTH = 31
FFN_CONV_WIDTH = 3
POOL_WINDOWS = (2, 4, 8, 16)
LANES = 128
SUBLANES = 8
MASK_VALUE = -1e30
FINITE_LIMIT = 3e38
BIAS_ROWS = 16

TOKEN_TILE = 512
ATTN_TILE = 512
ATTN_HEADS = 4
LOG2E = math.log2(math.e)
FFN_CHUNK = 256
FFN_TILES_PER_STEP = 2
CONV_ROWS = 64
VMEM_LIMIT = 56 * 1024 * 1024


def _const_spec(shape):
    nd = len(shape)
    return pl.BlockSpec(shape, lambda *_: (0,) * nd, pipeline_mode=pl.Buffered(1))


def _params():
    return pltpu.CompilerParams(dimension_semantics=("arbitrary",), vmem_limit_bytes=VMEM_LIMIT)


def _rms(x, g):
    ms = jnp.mean(x * x, axis=-1, keepdims=True)
    return x * lax.rsqrt(ms + EPS) * g


def _silu(x):
    return x * jax.nn.sigmoid(x)


def _ffn_kernel(x_ref, g_ref, wup_ref, dwp_ref, wdn_ref, o_ref, hb_ref, ua_ref, ub_ref, carry_ref, *, tm, nsub, nc,
                fc, steps_per_seq):
    i = pl.program_id(0)
    nslab = fc // LANES
    top = SUBLANES
    nwork = nsub * nc

    @pl.when(i % steps_per_seq == 0)
    def _():
        carry_ref[...] = jnp.zeros_like(carry_ref)

    def norm(sub):
        rows = slice(sub * tm, (sub + 1) * tm)
        x = x_ref[rows, :]
        hb_ref[rows, :] = _rms(x, g_ref[...]).astype(BF16)
        o_ref[rows, :] = x

    def split(w):
        if isinstance(w, int):
            return (w // nc) * tm, w % nc
        sub = sum((w >= k * nc).astype(jnp.int32) for k in range(1, nsub)) if nsub > 1 else 0
        return pl.multiple_of(sub * tm, tm), w - sub * nc

    def up(w, u_ref):
        r0, c = split(w)
        h = hb_ref[pl.ds(r0, tm), :]
        for p in range(2):
            u = jnp.dot(h, wup_ref[p, c], preferred_element_type=F32)
            for s in range(nslab):
                u_ref[p, s, top:top + tm, :] = u[:, s * LANES:(s + 1) * LANES]

    def gate_down(w, u_ref):
        r0, c = split(w)
        acts = []
        for s in range(nslab):
            ys = []
            for p in range(2):
                u_ref[p, s, 0:top, :] = carry_ref[p, c, s]
                carry_ref[p, c, s] = u_ref[p, s, tm:tm + top, :]
                t = dwp_ref[p, c, :, s * LANES:(s + 1) * LANES]
                y = (t[2:3] * u_ref[p, s, top:top + tm, :] + t[1:2] * u_ref[p, s, top - 1:top - 1 + tm, :]
                     + t[0:1] * u_ref[p, s, top - 2:top - 2 + tm, :]) + t[3:4]
                ys.append(y)
            val, gate = ys
            acts.append((_silu(gate) * val).astype(BF16))
        a = jnp.concatenate(acts, axis=-1)
        o_ref[pl.ds(r0, tm), :] += jnp.dot(a, wdn_ref[c], preferred_element_type=F32)

    norm(0)
    up(0, ua_ref)
    for sub in range(1, nsub):
        norm(sub)

    def pair(k, _):
        w = 2 * k
        up(w + 1, ub_ref)
        gate_down(w, ua_ref)
        up(w + 2, ua_ref)
        gate_down(w + 1, ub_ref)
        return 0

    lax.fori_loop(0, (nwork - 1) // 2, pair, 0)
    if nwork % 2 == 1:
        gate_down(nwork - 1, ua_ref)
    else:
        up(nwork - 1, ub_ref)
        gate_down(nwork - 2, ua_ref)
        gate_down(nwork - 1, ub_ref)


def _ffn(x2, seq, g, w_up, dw, dw_b, w_down):
    n, d = x2.shape
    f = w_down.shape[0]
    fc = FFN_CHUNK
    nc = f // fc
    tm = min(TOKEN_TILE, seq)
    assert f % fc == 0 and seq % tm == 0 and dw.shape[0] == FFN_CONV_WIDTH
    wup = w_up.astype(BF16).reshape(d, 2, nc, fc).transpose(1, 2, 0, 3)
    taps = jnp.concatenate([dw, dw_b[None, :], jnp.zeros((SUBLANES - 4, 2 * f), F32)], axis=0)
    dwp = taps.reshape(SUBLANES, 2, nc, fc).transpose(1, 2, 0, 3)
    wdn = w_down.astype(BF16).reshape(nc, fc, d)
    nslab = fc // LANES
    nsub = min(FFN_TILES_PER_STEP, seq // tm)
    ts = nsub * tm
    assert seq % ts == 0
    ubuf = pltpu.VMEM((2, nslab, SUBLANES + tm, LANES), F32)
    kern = functools.partial(_ffn_kernel, tm=tm, nsub=nsub, nc=nc, fc=fc, steps_per_seq=seq // ts)
    return pl.pallas_call(
        kern,
        out_shape=jax.ShapeDtypeStruct((n, d), F32),
        grid=(n // ts,),
        in_specs=[
            pl.BlockSpec((ts, d), lambda i: (i, 0)),
            _const_spec((1, d)),
            _const_spec((2, nc, d, fc)),
            _const_spec((2, nc, SUBLANES, fc)),
            _const_spec((nc, fc, d)),
        ],
        out_specs=pl.BlockSpec((ts, d), lambda i: (i, 0)),
        scratch_shapes=[
            pltpu.VMEM((ts, d), BF16),
            ubuf,
            ubuf,
            pltpu.VMEM((2, nc, nslab, SUBLANES, LANES), F32),
        ],
        compiler_params=_params(),
        name="conv_ffn",
    )(x2, g[None, :], wup, dwp, wdn)


def _conf_kernel(x_ref, g_ref, win_ref, bin_ref, dws_ref, lng_ref, lnb_ref, wout_ref, bout_ref, o_ref, vbuf_ref,
                 ybuf_ref, *, tm, d, tiles_per_seq):
    i = pl.program_id(0)
    halo = 32
    nslab = d // LANES
    x = x_ref[...]
    hb = _rms(x, g_ref[...]).astype(BF16)
    p = jnp.dot(hb, win_ref[...], preferred_element_type=F32) + bin_ref[...]
    v = p[:, :d] * jax.nn.sigmoid(p[:, d:])

    @pl.when(i % tiles_per_seq == 0)
    def _():
        vbuf_ref[:, 0:halo, :] = jnp.zeros((nslab, halo, LANES), F32)

    @pl.when(i % tiles_per_seq != 0)
    def _():
        vbuf_ref[:, 0:halo, :] = vbuf_ref[:, tm:tm + halo, :]

    for j in range(nslab):
        vbuf_ref[j, halo:halo + tm, :] = v[:, j * LANES:(j + 1) * LANES]

    rb = CONV_ROWS
    first = halo - (CONV_WIDTH - 1)

    def slab(j, _):
        def rblock(r, _):
            base = pl.multiple_of(r * rb, rb)
            acc = jnp.broadcast_to(dws_ref[j, CONV_WIDTH:CONV_WIDTH + 1, :], (rb, LANES))
            for k in range(CONV_WIDTH):
                acc = acc + dws_ref[j, k:k + 1, :] * vbuf_ref[j, pl.ds(base + first + k, rb), :]
            ybuf_ref[j, pl.ds(base, rb), :] = acc
            return 0

        lax.fori_loop(0, tm // rb, rblock, 0)
        return 0

    lax.fori_loop(0, nslab, slab, 0)

    y = jnp.concatenate([ybuf_ref[j] for j in range(nslab)], axis=-1)
    mu = jnp.mean(y, axis=-1, keepdims=True)
    yc = y - mu
    var = jnp.mean(yc * yc, axis=-1, keepdims=True)
    yn = yc * lax.rsqrt(var + EPS) * lng_ref[...] + lnb_ref[...]
    sw = _silu(yn).astype(BF16)
    o_ref[...] = x + jnp.dot(sw, wout_ref[...], preferred_element_type=F32) + bout_ref[...]


def _conformer(x2, seq, g, w_in, b_in, dw, dw_b, ln_g, ln_b, w_out, b_out):
    n, d = x2.shape
    tm = min(TOKEN_TILE, seq)
    nslab = d // LANES
    halo = 32
    assert dw.shape[0] == CONV_WIDTH and seq % tm == 0 and tm % CONV_ROWS == 0
    taps = jnp.concatenate([dw, dw_b[None, :]], axis=0)
    dws = taps.reshape(CONV_WIDTH + 1, nslab, LANES).transpose(1, 0, 2)
    kern = functools.partial(_conf_kernel, tm=tm, d=d, tiles_per_seq=seq // tm)
    row = lambda a: a[None, :]
    return pl.pallas_call(
        kern,
        out_shape=jax.ShapeDtypeStruct((n, d), F32),
        grid=(n // tm,),
        in_specs=[
            pl.BlockSpec((tm, d), lambda i: (i, 0)),
            _const_spec((1, d)),
            _const_spec((d, 2 * d)),
            _const_spec((1, 2 * d)),
            _const_spec((nslab, CONV_WIDTH + 1, LANES)),
            _const_spec((1, d)),
            _const_spec((1, d)),
            _const_spec((d, d)),
            _const_spec((1, d)),
        ],
        out_specs=pl.BlockSpec((tm, d), lambda i: (i, 0)),
        scratch_shapes=[
            pltpu.VMEM((nslab, halo + tm, LANES), F32),
            pltpu.VMEM((nslab, tm, LANES), F32),
        ],
        compiler_params=_params(),
        name="conformer_conv",
    )(x2, row(g), w_in.astype(BF16), row(b_in), dws, row(ln_g), row(ln_b), w_out.astype(BF16), row(b_out))


def _pool_kernel(x_ref, g_ref, pw_ref, pb_ref, sc_ref, o_ref, hbuf_ref, *, tm, d, tiles_per_seq):
    i = pl.program_id(0)
    halo = 16
    cg = d // len(POOL_WINDOWS)
    x = x_ref[...]
    h = _rms(x, g_ref[...])

    @pl.when(i % tiles_per_seq == 0)
    def _():
        hbuf_ref[0:halo, :] = jnp.zeros((halo, d), F32)

    @pl.when(i % tiles_per_seq != 0)
    def _():
        hbuf_ref[0:halo, :] = hbuf_ref[tm:tm + halo, :]

    hbuf_ref[halo:halo + tm, :] = h
    pos = (i % tiles_per_seq) * tm + lax.broadcasted_iota(jnp.int32, (tm, cg), 0)
    outs = []
    for gi, w in enumerate(POOL_WINDOWS):
        cols = slice(gi * cg, (gi + 1) * cg)
        hg = h[:, cols]
        s = hg
        for dlt in range(1, w):
            s = s + hbuf_ref[halo - dlt:halo - dlt + tm, cols]
        cnt = jnp.minimum(pos + 1, w).astype(F32)
        pg = (s / cnt - hg).astype(BF16)
        yg = jnp.dot(pg, pw_ref[gi], preferred_element_type=F32) + pb_ref[gi]
        outs.append(yg)
    y = jnp.concatenate(outs, axis=-1)
    o_ref[...] = x + y * sc_ref[...]


def _pool(x2, seq, g, pw, pb, scale):
    n, d = x2.shape
    tm = min(TOKEN_TILE, seq)
    ng = len(POOL_WINDOWS)
    cg = d // ng
    assert seq % tm == 0
    kern = functools.partial(_pool_kernel, tm=tm, d=d, tiles_per_seq=seq // tm)
    return pl.pallas_call(
        kern,
        out_shape=jax.ShapeDtypeStruct((n, d), F32),
        grid=(n // tm,),
        in_specs=[
            pl.BlockSpec((tm, d), lambda i: (i, 0)),
            _const_spec((1, d)),
            _const_spec((ng, cg, cg)),
            _const_spec((ng, 1, cg)),
            _const_spec((1, d)),
        ],
        out_specs=pl.BlockSpec((tm, d), lambda i: (i, 0)),
        scratch_shapes=[pltpu.VMEM((16 + tm, d), F32)],
        compiler_params=_params(),
        name="pool_mixer",
    )(x2, g[None, :], pw.astype(BF16), pb[:, None, :], scale[None, :])


def _fox_proj_kernel(x_ref, g_ref, wq_ref, wkt_ref, wv_ref, wft_ref, bf_ref, e_ref, qg_ref, kg_ref, q_ref, kt_ref,
                     v_ref, c_ref, carry_ref, *, tm, d, tiles_per_seq):
    i = pl.program_id(0)
    hb = _rms(x_ref[...], g_ref[...]).astype(BF16)
    nt = (((1,), (1,)), ((), ()))

    q = jnp.dot(hb, wq_ref[...], preferred_element_type=F32)
    qms = jnp.dot((q * q).astype(BF16), e_ref[...], preferred_element_type=F32)
    q_ref[...] = (q * lax.rsqrt(qms + EPS) * qg_ref[...]).astype(BF16)

    kt = lax.dot_general(wkt_ref[...], hb, nt, preferred_element_type=F32)
    kt3 = kt.reshape(N_HEADS, HEAD_DIM, tm)
    kms = jnp.mean(kt3 * kt3, axis=1, keepdims=True)
    kgain = jnp.tile(kg_ref[...], (1, tm // LANES))
    kt_ref[...] = ((kt3 * lax.rsqrt(kms + EPS)).reshape(d, tm) * kgain).astype(BF16)

    v_ref[...] = jnp.dot(hb, wv_ref[...], preferred_element_type=F32).astype(BF16)

    z = lax.dot_general(wft_ref[...], hb, nt, preferred_element_type=F32) + bf_ref[...]
    logf = jnp.minimum(z, 0.0) - jnp.log1p(jnp.exp(-jnp.abs(z)))
    lane = lax.broadcasted_iota(jnp.int32, logf.shape, 1)
    c = logf
    sh = 1
    while sh < tm:
        c = c + jnp.where(lane >= sh, pltpu.roll(c, sh, axis=1), 0.0)
        sh *= 2

    @pl.when(i % tiles_per_seq == 0)
    def _():
        carry_ref[...] = jnp.zeros_like(carry_ref)

    c = c + jnp.tile(carry_ref[...], (1, tm // LANES))
    c_ref[...] = c
    carry_ref[...] = jnp.broadcast_to(c[:, tm - 1:tm], carry_ref.shape)


def _fox_proj(x2, seq, g, w_in, b_f, q_gain, k_gain):
    n, d = x2.shape
    tm = min(TOKEN_TILE, seq)
    nh = N_HEADS
    assert d == nh * HEAD_DIM and seq % tm == 0
    wq = w_in[:, :d].astype(BF16)
    wkt = w_in[:, d:2 * d].T.astype(BF16)
    wv = w_in[:, 2 * d:3 * d].astype(BF16)
    wft = w_in[:, 3 * d:].T.astype(BF16)
    bfb = jnp.broadcast_to(b_f[:, None], (nh, tm)).astype(F32)
    head = jnp.arange(d) // HEAD_DIM
    e = (head[:, None] == head[None, :]).astype(BF16) * (1.0 / HEAD_DIM)
    qg = (jnp.tile(q_gain, nh) * (LOG2E / math.sqrt(HEAD_DIM)))[None, :]
    kg = jnp.broadcast_to(jnp.tile(k_gain, nh)[:, None], (d, LANES)).astype(F32)
    kern = functools.partial(_fox_proj_kernel, tm=tm, d=d, tiles_per_seq=seq // tm)
    return pl.pallas_call(
        kern,
        out_shape=(
            jax.ShapeDtypeStruct((n, d), BF16),
            jax.ShapeDtypeStruct((d, n), BF16),
            jax.ShapeDtypeStruct((n, d), BF16),
            jax.ShapeDtypeStruct((nh, n), F32),
        ),
        grid=(n // tm,),
        in_specs=[
            pl.BlockSpec((tm, d), lambda i: (i, 0)),
            _const_spec((1, d)),
            _const_spec((d, d)),
            _const_spec((d, d)),
            _const_spec((d, d)),
            _const_spec((nh, d)),
            _const_spec((nh, tm)),
            _const_spec((d, d)),
            _const_spec((1, d)),
            _const_spec((d, LANES)),
        ],
        out_specs=(
            pl.BlockSpec((tm, d), lambda i: (i, 0)),
            pl.BlockSpec((d, tm), lambda i: (0, i)),
            pl.BlockSpec((tm, d), lambda i: (i, 0)),
            pl.BlockSpec((nh, tm), lambda i: (0, i)),
        ),
        scratch_shapes=[pltpu.VMEM((nh, LANES), F32)],
        compiler_params=_params(),
        name="fox_proj",
    )(x2, g[None, :], wq, wkt, wv, wft, bfb, e, qg, kg)


def _split3(a):
    hi = a.astype(BF16).astype(F32)
    r = a - hi
    mid = r.astype(BF16).astype(F32)
    lo = (r - mid).astype(BF16).astype(F32)
    return hi, mid, lo


def _attn_kernel(q_ref, kt_ref, v_ref, c_ref, o_ref, krhs_ref, vext_ref, accx_ref, m_ref, l_ref, acc_ref, *, t, nblk,
                 nhead):
    g = pl.program_id(1)
    i = pl.program_id(2)
    kdim = 2 * LANES
    npair = nhead // 2

    @pl.when(i == 0)
    def _():
        brow = lax.broadcasted_iota(jnp.int32, (BIAS_ROWS, t), 0)
        for e in range(nhead):
            crow = c_ref[pl.ds(nhead * g + e, 1), :] * LOG2E
            hi, mid, lo = (jnp.broadcast_to(a, (BIAS_ROWS, a.shape[1])) for a in _split3(-crow))
            for jb in range(nblk):
                cols = slice(jb * t, (jb + 1) * t)
                for eo in range(2):
                    rows = slice(eo * HEAD_DIM, (eo + 1) * HEAD_DIM)
                    if eo == e % 2:
                        src = slice((e // 2) * LANES + eo * HEAD_DIM, (e // 2) * LANES + (eo + 1) * HEAD_DIM)
                        krhs_ref[e, jb, rows, :] = kt_ref[src, cols]
                    else:
                        krhs_ref[e, jb, rows, :] = jnp.zeros((HEAD_DIM, t), BF16)
                bias = jnp.where(brow == 0, hi[:, cols], jnp.where(brow == 1, mid[:, cols],
                                 jnp.where(brow == 2, lo[:, cols], 0.0)))
                krhs_ref[e, jb, LANES:LANES + BIAS_ROWS, :] = bias.astype(BF16)
                krhs_ref[e, jb, LANES + BIAS_ROWS:kdim, :] = jnp.zeros((kdim - LANES - BIAS_ROWS, t), BF16)

        for pr in range(npair):
            vext_ref[pr, :, 0:LANES] = v_ref[:, pr * LANES:(pr + 1) * LANES]
            vext_ref[pr, :, LANES:2 * LANES] = jnp.ones((nblk * t, LANES), BF16)

    ones = jnp.ones((t, LANES), BF16)
    lhs = [jnp.concatenate([q_ref[:, pr * LANES:(pr + 1) * LANES], ones], axis=1) for pr in range(npair)]
    lane = lax.broadcasted_iota(jnp.int32, (t, LANES), 1)

    def causal(s):
        r = lax.broadcasted_iota(jnp.int32, (t, t), 0)
        cc = lax.broadcasted_iota(jnp.int32, (t, t), 1)
        return jnp.where(cc <= r, s, MASK_VALUE)

    for e in range(nhead):
        pr = e // 2
        s = causal(jnp.dot(lhs[pr], krhs_ref[e, i], preferred_element_type=F32))
        m = jnp.max(s, axis=1, keepdims=True)
        m_ref[e] = jnp.broadcast_to(m, (t, LANES))
        p = jnp.exp2(s - m).astype(BF16)
        accx_ref[e] = jnp.dot(p, vext_ref[pr, pl.ds(pl.multiple_of(i * t, t), t), :], preferred_element_type=F32)

    def fast(j, _):
        off = pl.multiple_of(j * t, t)
        for e in range(nhead):
            pr = e // 2
            s = jnp.dot(lhs[pr], krhs_ref[e, j], preferred_element_type=F32)
            p = jnp.exp2(s - jnp.tile(m_ref[e], (1, t // LANES))).astype(BF16)
            accx_ref[e] += jnp.dot(p, vext_ref[pr, pl.ds(off, t), :], preferred_element_type=F32)
        return 0

    lax.fori_loop(0, i, fast, 0)
    lmax = jnp.max(accx_ref[:, :, LANES:2 * LANES])
    finite = lmax < FINITE_LIMIT

    @pl.when(finite)
    def _():
        outs = [jnp.where(lane < HEAD_DIM, accx_ref[2 * pr, :, 0:LANES] / accx_ref[2 * pr, :, LANES:2 * LANES],
                          accx_ref[2 * pr + 1, :, 0:LANES] / accx_ref[2 * pr + 1, :, LANES:2 * LANES])
                for pr in range(npair)]
        o_ref[...] = jnp.concatenate(outs, axis=1).astype(BF16)

    @pl.when(jnp.logical_not(finite))
    def _():
        m_ref[...] = jnp.full(m_ref.shape, MASK_VALUE, F32)
        l_ref[...] = jnp.zeros(l_ref.shape, F32)
        acc_ref[...] = jnp.zeros(acc_ref.shape, F32)

        def step(j, masked):
            off = pl.multiple_of(j * t, t)
            for e in range(nhead):
                pr = e // 2
                s = jnp.dot(lhs[pr], krhs_ref[e, j], preferred_element_type=F32)
                if masked:
                    s = causal(s)
                m_prev = m_ref[e]
                m_new = jnp.maximum(m_prev, jnp.max(s, axis=1, keepdims=True))
                alpha = jnp.exp2(m_prev - m_new)
                p = jnp.exp2(s - jnp.tile(m_new, (1, t // LANES)))
                l_ref[e] = alpha * l_ref[e] + jnp.sum(p, axis=1, keepdims=True)
                vblk = v_ref[pl.ds(off, t), pr * LANES:(pr + 1) * LANES]
                acc_ref[e] = alpha * acc_ref[e] + jnp.dot(p.astype(BF16), vblk, preferred_element_type=F32)
                m_ref[e] = m_new

        def body(j, _):
            step(j, False)
            return 0

        lax.fori_loop(0, i, body, 0)
        step(i, True)
        outs = [jnp.where(lane < HEAD_DIM, acc_ref[2 * pr] / l_ref[2 * pr], acc_ref[2 * pr + 1] / l_ref[2 * pr + 1])
                for pr in range(npair)]
        o_ref[...] = jnp.concatenate(outs, axis=1).astype(BF16)


def _attention(q, kt, v, c, bsz, seq):
    n, d = q.shape
    t = min(ATTN_TILE, seq)
    nblk = seq // t
    nhead = ATTN_HEADS
    w = nhead * HEAD_DIM
    assert seq % t == 0 and 2 * HEAD_DIM == LANES and d % w == 0 and w % LANES == 0
    kern = functools.partial(_attn_kernel, t=t, nblk=nblk, nhead=nhead)
    return pl.pallas_call(
        kern,
        out_shape=jax.ShapeDtypeStruct((n, d), BF16),
        grid=(bsz, d // w, nblk),
        in_specs=[
            pl.BlockSpec((t, w), lambda b, g, i: (b * nblk + i, g)),
            pl.BlockSpec((w, seq), lambda b, g, i: (g, b)),
            pl.BlockSpec((seq, w), lambda b, g, i: (b, g)),
            pl.BlockSpec((N_HEADS, seq), lambda b, g, i: (0, b)),
        ],
        out_specs=pl.BlockSpec((t, w), lambda b, g, i: (b * nblk + i, g)),
        scratch_shapes=[
            pltpu.VMEM((nhead, nblk, 2 * LANES, t), BF16),
            pltpu.VMEM((nhead // 2, seq, 2 * LANES), BF16),
            pltpu.VMEM((nhead, t, 2 * LANES), F32),
            pltpu.VMEM((nhead, t, LANES), F32),
            pltpu.VMEM((nhead, t, LANES), F32),
            pltpu.VMEM((nhead, t, LANES), F32),
        ],
        compiler_params=pltpu.CompilerParams(dimension_semantics=("arbitrary",) * 3, vmem_limit_bytes=VMEM_LIMIT),
        name="fox_attention",
    )(q, kt, v, c)


def _out_proj_kernel(x_ref, o_ref, w_ref, y_ref):
    y_ref[...] = x_ref[...] + jnp.dot(o_ref[...], w_ref[...], preferred_element_type=F32)


def _out_proj(x2, o, w_o):
    n, d = x2.shape
    tm = min(TOKEN_TILE, n)
    return pl.pallas_call(
        _out_proj_kernel,
        out_shape=jax.ShapeDtypeStruct((n, d), F32),
        grid=(n // tm,),
        in_specs=[
            pl.BlockSpec((tm, d), lambda i: (i, 0)),
            pl.BlockSpec((tm, d), lambda i: (i, 0)),
            _const_spec((d, d)),
        ],
        out_specs=pl.BlockSpec((tm, d), lambda i: (i, 0)),
        compiler_params=_params(),
        name="fox_out_proj",
    )(x2, o, w_o.astype(BF16))


def _fox(x2, bsz, seq, g, w_in, b_f, q_gain, k_gain, w_o):
    q, kt, v, c = _fox_proj(x2, seq, g, w_in, b_f, q_gain, k_gain)
    o = _attention(q, kt, v, c, bsz, seq)
    return _out_proj(x2, o, w_o)


def kernel(x, norm_mix, norm_ffn, conv_w_in, conv_b_in, conv_dw, conv_dw_b, conv_ln_g, conv_ln_b, conv_w_out,
           conv_b_out, pool_w, pool_b, pool_scale, fox_w_in, fox_b_f, fox_q_gain, fox_k_gain, fox_w_o, ffn_w_up,
           ffn_dw, ffn_dw_b, ffn_w_down):
    bsz, seq, d = x.shape
    depth = norm_mix.shape[0]
    x2 = x.reshape(bsz * seq, d)
    for i in range(depth):
        j = i // 3
        kind = i % 3
        if kind == 0:
            x2 = _conformer(x2, seq, norm_mix[i], conv_w_in[j], conv_b_in[j], conv_dw[j], conv_dw_b[j],
                            conv_ln_g[j], conv_ln_b[j], conv_w_out[j], conv_b_out[j])
        elif kind == 1:
            x2 = _pool(x2, seq, norm_mix[i], pool_w[j], pool_b[j], pool_scale[j])
        else:
            x2 = _fox(x2, bsz, seq, norm_mix[i], fox_w_in[j], fox_b_f[j], fox_q_gain[j], fox_k_gain[j], fox_w_o[j])
        x2 = _ffn(x2, seq, norm_ffn[i], ffn_w_up[i], ffn_dw[i], ffn_dw_b[i], ffn_w_down[i])
    return x2.reshape(bsz, seq, d)
```

```python
import functools
import math

import jax
import jax.numpy as jnp
from jax import lax
from jax.experimental import pallas as pl
from jax.experimental.pallas import tpu as pltpu

F32 = jnp.float32
BF16 = jnp.bfloat16

EPS = 1e-6
N_HEADS = 16
HEAD_DIM = 64
CONV_WIDTH = 31
FFN_CONV_WIDTH = 3
POOL_WINDOWS = (2, 4, 8, 16)
LANES = 128
SUBLANES = 8
MASK_VALUE = -1e30
FINITE_LIMIT = 3e38
UNDERFLOW_EXP2 = 160.0
QK_BOUND_SLACK = 1.02
BIAS_ROWS = 16

TOKEN_TILE = 512
ATTN_TILE = 512
ATTN_HEADS = 4
LOG2E = math.log2(math.e)
FFN_CHUNK = 256
FFN_TILES_PER_STEP = 2
CONV_ROWS = 128
CONV_HALO = 32
CONF_TILES_PER_STEP = 2
PROJ_CHUNK = 256
VMEM_LIMIT = 56 * 1024 * 1024


def _const_spec(shape):
    nd = len(shape)
    return pl.BlockSpec(shape, lambda *_: (0,) * nd, pipeline_mode=pl.Buffered(1))


def _params():
    return pltpu.CompilerParams(dimension_semantics=("arbitrary",), vmem_limit_bytes=VMEM_LIMIT)


def _rms(x, g):
    ms = jnp.mean(x * x, axis=-1, keepdims=True)
    return x * lax.rsqrt(ms + EPS) * g


def _silu(x):
    return x * jax.nn.sigmoid(x)


def _ffn_kernel(x_ref, g_ref, wup_ref, dwp_ref, wdn_ref, o_ref, hb_ref, ua_ref, ub_ref, carry_ref, *, tm, nsub, nc,
                fc, steps_per_seq):
    i = pl.program_id(0)
    nslab = fc // LANES
    top = SUBLANES
    nwork = nsub * nc

    @pl.when(i % steps_per_seq == 0)
    def _():
        carry_ref[...] = jnp.zeros_like(carry_ref)

    def norm(sub):
        rows = slice(sub * tm, (sub + 1) * tm)
        x = x_ref[rows, :]
        hb_ref[rows, :] = _rms(x, g_ref[...]).astype(BF16)
        o_ref[rows, :] = x

    def split(w):
        if isinstance(w, int):
            return (w // nc) * tm, w % nc
        sub = sum((w >= k * nc).astype(jnp.int32) for k in range(1, nsub)) if nsub > 1 else 0
        return pl.multiple_of(sub * tm, tm), w - sub * nc

    def up(w, u_ref):
        r0, c = split(w)
        h = hb_ref[pl.ds(r0, tm), :]
        for p in range(2):
            u = jnp.dot(h, wup_ref[p, c], preferred_element_type=F32)
            for s in range(nslab):
                u_ref[p, s, top:top + tm, :] = u[:, s * LANES:(s + 1) * LANES]

    def gate_down(w, u_ref):
        r0, c = split(w)
        acts = []
        for s in range(nslab):
            ys = []
            for p in range(2):
                u_ref[p, s, 0:top, :] = carry_ref[p, c, s]
                carry_ref[p, c, s] = u_ref[p, s, tm:tm + top, :]
                t = dwp_ref[p, c, :, s * LANES:(s + 1) * LANES]
                y = (t[2:3] * u_ref[p, s, top:top + tm, :] + t[1:2] * u_ref[p, s, top - 1:top - 1 + tm, :]
                     + t[0:1] * u_ref[p, s, top - 2:top - 2 + tm, :]) + t[3:4]
                ys.append(y)
            val, gate = ys
            acts.append((_silu(gate) * val).astype(BF16))
        a = jnp.concatenate(acts, axis=-1)
        o_ref[pl.ds(r0, tm), :] += jnp.dot(a, wdn_ref[c], preferred_element_type=F32)

    norm(0)
    up(0, ua_ref)
    for sub in range(1, nsub):
        norm(sub)

    def pair(k, _):
        w = 2 * k
        up(w + 1, ub_ref)
        gate_down(w, ua_ref)
        up(w + 2, ua_ref)
        gate_down(w + 1, ub_ref)
        return 0

    lax.fori_loop(0, (nwork - 1) // 2, pair, 0)
    if nwork % 2 == 1:
        gate_down(nwork - 1, ua_ref)
    else:
        up(nwork - 1, ub_ref)
        gate_down(nwork - 2, ua_ref)
        gate_down(nwork - 1, ub_ref)


def _ffn(x2, seq, g, w_up, dw, dw_b, w_down):
    n, d = x2.shape
    f = w_down.shape[0]
    fc = FFN_CHUNK
    nc = f // fc
    tm = min(TOKEN_TILE, seq)
    assert f % fc == 0 and seq % tm == 0 and dw.shape[0] == FFN_CONV_WIDTH
    wup = w_up.astype(BF16).reshape(d, 2, nc, fc).transpose(1, 2, 0, 3)
    taps = jnp.concatenate([dw, dw_b[None, :], jnp.zeros((SUBLANES - 4, 2 * f), F32)], axis=0)
    dwp = taps.reshape(SUBLANES, 2, nc, fc).transpose(1, 2, 0, 3)
    wdn = w_down.astype(BF16).reshape(nc, fc, d)
    nslab = fc // LANES
    nsub = min(FFN_TILES_PER_STEP, seq // tm)
    ts = nsub * tm
    assert seq % ts == 0
    ubuf = pltpu.VMEM((2, nslab, SUBLANES + tm, LANES), F32)
    kern = functools.partial(_ffn_kernel, tm=tm, nsub=nsub, nc=nc, fc=fc, steps_per_seq=seq // ts)
    return pl.pallas_call(
        kern,
        out_shape=jax.ShapeDtypeStruct((n, d), F32),
        grid=(n // ts,),
        in_specs=[
            pl.BlockSpec((ts, d), lambda i: (i, 0)),
            _const_spec((1, d)),
            _const_spec((2, nc, d, fc)),
            _const_spec((2, nc, SUBLANES, fc)),
            _const_spec((nc, fc, d)),
        ],
        out_specs=pl.BlockSpec((ts, d), lambda i: (i, 0)),
        scratch_shapes=[
            pltpu.VMEM((ts, d), BF16),
            ubuf,
            ubuf,
            pltpu.VMEM((2, nc, nslab, SUBLANES, LANES), F32),
        ],
        compiler_params=_params(),
        name="conv_ffn",
    )(x2, g[None, :], wup, dwp, wdn)


def _conf_kernel(x_ref, g_ref, win_ref, bin_ref, dws_ref, lng_ref, lnb_ref, wout_ref, bout_ref, o_ref, hb_ref,
                 vbuf_ref, ybuf_ref, pbuf_ref, sw_ref, *, tm, nsub, d, steps_per_seq):
    i = pl.program_id(0)
    halo = CONV_HALO
    nslab = d // LANES
    nch = 2 * d // PROJ_CHUNK
    rb = CONV_ROWS
    first = halo - (CONV_WIDTH - 1)

    @pl.when(i % steps_per_seq == 0)
    def _():
        vbuf_ref[:, 0:halo, :] = jnp.zeros((nslab, halo, LANES), F32)

    @pl.when(i % steps_per_seq != 0)
    def _():
        vbuf_ref[:, 0:halo, :] = vbuf_ref[:, nsub * tm:nsub * tm + halo, :]

    def rows(sub):
        return slice(sub * tm, (sub + 1) * tm)

    def norm(sub):
        hb_ref[rows(sub), :] = _rms(x_ref[rows(sub), :], g_ref[...]).astype(BF16)

    def in_chunk(sub, c):
        pbuf_ref[c] = jnp.dot(hb_ref[rows(sub), :], win_ref[c], preferred_element_type=F32) + bin_ref[c]

    def glu(sub):
        half = PROJ_CHUNK // LANES
        for s in range(nslab):
            lanes = slice((s % half) * LANES, (s % half + 1) * LANES)
            a = pbuf_ref[s // half, :, lanes]
            gate = pbuf_ref[nch // 2 + s // half, :, lanes]
            vbuf_ref[s, halo + sub * tm:halo + (sub + 1) * tm, :] = a * jax.nn.sigmoid(gate)

    def conv_slab(sub, j):
        for r in range(tm // rb):
            base = sub * tm + r * rb
            acc = jnp.broadcast_to(dws_ref[j, CONV_WIDTH:CONV_WIDTH + 1, :], (rb, LANES))
            for k in range(CONV_WIDTH):
                acc = acc + dws_ref[j, k:k + 1, :] * vbuf_ref[j, base + first + k:base + first + k + rb, :]
            ybuf_ref[j, base:base + rb, :] = acc

    def layer_norm_swish(sub):
        y = jnp.concatenate([ybuf_ref[j, rows(sub), :] for j in range(nslab)], axis=-1)
        mu = jnp.mean(y, axis=-1, keepdims=True)
        yc = y - mu
        var = jnp.mean(yc * yc, axis=-1, keepdims=True)
        yn = yc * lax.rsqrt(var + EPS) * lng_ref[...] + lnb_ref[...]
        sw_ref[...] = _silu(yn).astype(BF16)

    def out_chunk(sub, c):
        cols = slice(c * PROJ_CHUNK, (c + 1) * PROJ_CHUNK)
        o_ref[rows(sub), cols] = (x_ref[rows(sub), cols] + bout_ref[:, cols]
                                  + jnp.dot(sw_ref[...], wout_ref[:, cols], preferred_element_type=F32))

    nout = d // PROJ_CHUNK
    for sub in range(nsub):
        norm(sub)
    for c in range(nch):
        in_chunk(0, c)
    glu(0)
    for sub in range(nsub):
        if sub > 0:
            layer_norm_swish(sub - 1)
        for j in range(nslab):
            conv_slab(sub, j)
            if sub + 1 < nsub:
                in_chunk(sub + 1, j)
            if sub > 0 and j % (nslab // nout) == 0:
                out_chunk(sub - 1, j // (nslab // nout))
        if sub + 1 < nsub:
            glu(sub + 1)
    layer_norm_swish(nsub - 1)
    for c in range(nout):
        out_chunk(nsub - 1, c)


def _conformer(x2, seq, g, w_in, b_in, dw, dw_b, ln_g, ln_b, w_out, b_out):
    n, d = x2.shape
    tm = min(TOKEN_TILE, seq)
    nsub = min(CONF_TILES_PER_STEP, seq // tm)
    ts = nsub * tm
    nslab = d // LANES
    nch = 2 * d // PROJ_CHUNK
    assert dw.shape[0] == CONV_WIDTH and seq % ts == 0 and tm % CONV_ROWS == 0 and nch == nslab
    taps = jnp.concatenate([dw, dw_b[None, :]], axis=0)
    dws = taps.reshape(CONV_WIDTH + 1, nslab, LANES).transpose(1, 0, 2)
    win = w_in.astype(BF16).reshape(d, nch, PROJ_CHUNK).transpose(1, 0, 2)
    bin_ = b_in.reshape(nch, 1, PROJ_CHUNK)
    kern = functools.partial(_conf_kernel, tm=tm, nsub=nsub, d=d, steps_per_seq=seq // ts)
    row = lambda a: a[None, :]
    return pl.pallas_call(
        kern,
        out_shape=jax.ShapeDtypeStruct((n, d), F32),
        grid=(n // ts,),
        in_specs=[
            pl.BlockSpec((ts, d), lambda i: (i, 0)),
            _const_spec((1, d)),
            _const_spec((nch, d, PROJ_CHUNK)),
            _const_spec((nch, 1, PROJ_CHUNK)),
            _const_spec((nslab, CONV_WIDTH + 1, LANES)),
            _const_spec((1, d)),
            _const_spec((1, d)),
            _const_spec((d, d)),
            _const_spec((1, d)),
        ],
        out_specs=pl.BlockSpec((ts, d), lambda i: (i, 0)),
        scratch_shapes=[
            pltpu.VMEM((ts, d), BF16),
            pltpu.VMEM((nslab, CONV_HALO + ts, LANES), F32),
            pltpu.VMEM((nslab, ts, LANES), F32),
            pltpu.VMEM((nch, tm, PROJ_CHUNK), F32),
            pltpu.VMEM((tm, d), BF16),
        ],
        compiler_params=_params(),
        name="conformer_conv",
    )(x2, row(g), win, bin_, dws, row(ln_g), row(ln_b), w_out.astype(BF16), row(b_out))


def _pool_kernel(x_ref, g_ref, pw_ref, pb_ref, sc_ref, o_ref, hbuf_ref, *, tm, d, tiles_per_seq):
    i = pl.program_id(0)
    halo = 16
    cg = d // len(POOL_WINDOWS)
    x = x_ref[...]
    h = _rms(x, g_ref[...])

    @pl.when(i % tiles_per_seq == 0)
    def _():
        hbuf_ref[0:halo, :] = jnp.zeros((halo, d), F32)

    @pl.when(i % tiles_per_seq != 0)
    def _():
        hbuf_ref[0:halo, :] = hbuf_ref[tm:tm + halo, :]

    hbuf_ref[halo:halo + tm, :] = h
    pos = (i % tiles_per_seq) * tm + lax.broadcasted_iota(jnp.int32, (tm, cg), 0)
    outs = []
    for gi, w in enumerate(POOL_WINDOWS):
        cols = slice(gi * cg, (gi + 1) * cg)
        hg = h[:, cols]
        s = hg
        for dlt in range(1, w):
            s = s + hbuf_ref[halo - dlt:halo - dlt + tm, cols]
        cnt = jnp.minimum(pos + 1, w).astype(F32)
        pg = (s / cnt - hg).astype(BF16)
        yg = jnp.dot(pg, pw_ref[gi], preferred_element_type=F32) + pb_ref[gi]
        outs.append(yg)
    y = jnp.concatenate(outs, axis=-1)
    o_ref[...] = x + y * sc_ref[...]


def _pool(x2, seq, g, pw, pb, scale):
    n, d = x2.shape
    tm = min(TOKEN_TILE, seq)
    ng = len(POOL_WINDOWS)
    cg = d // ng
    assert seq % tm == 0
    kern = functools.partial(_pool_kernel, tm=tm, d=d, tiles_per_seq=seq // tm)
    return pl.pallas_call(
        kern,
        out_shape=jax.ShapeDtypeStruct((n, d), F32),
        grid=(n // tm,),
        in_specs=[
            pl.BlockSpec((tm, d), lambda i: (i, 0)),
            _const_spec((1, d)),
            _const_spec((ng, cg, cg)),
            _const_spec((ng, 1, cg)),
            _const_spec((1, d)),
        ],
        out_specs=pl.BlockSpec((tm, d), lambda i: (i, 0)),
        scratch_shapes=[pltpu.VMEM((16 + tm, d), F32)],
        compiler_params=_params(),
        name="pool_mixer",
    )(x2, g[None, :], pw.astype(BF16), pb[:, None, :], scale[None, :])


def _fox_proj_kernel(x_ref, g_ref, wq_ref, wkt_ref, wv_ref, wft_ref, bf_ref, e_ref, qg_ref, kg_ref, q_ref, kt_ref,
                     v_ref, c_ref, carry_ref, *, tm, d, tiles_per_seq):
    i = pl.program_id(0)
    hb = _rms(x_ref[...], g_ref[...]).astype(BF16)
    nt = (((1,), (1,)), ((), ()))

    q = jnp.dot(hb, wq_ref[...], preferred_element_type=F32)
    qms = jnp.dot((q * q).astype(BF16), e_ref[...], preferred_element_type=F32)
    q_ref[...] = (q * lax.rsqrt(qms + EPS) * qg_ref[...]).astype(BF16)

    kt = lax.dot_general(wkt_ref[...], hb, nt, preferred_element_type=F32)
    kt3 = kt.reshape(N_HEADS, HEAD_DIM, tm)
    kms = jnp.mean(kt3 * kt3, axis=1, keepdims=True)
    kgain = jnp.tile(kg_ref[...], (1, tm // LANES))
    kt_ref[...] = ((kt3 * lax.rsqrt(kms + EPS)).reshape(d, tm) * kgain).astype(BF16)

    v_ref[...] = jnp.dot(hb, wv_ref[...], preferred_element_type=F32).astype(BF16)

    z = lax.dot_general(wft_ref[...], hb, nt, preferred_element_type=F32) + bf_ref[...]
    logf = jnp.minimum(z, 0.0) - jnp.log1p(jnp.exp(-jnp.abs(z)))
    lane = lax.broadcasted_iota(jnp.int32, logf.shape, 1)
    c = logf
    sh = 1
    while sh < tm:
        c = c + jnp.where(lane >= sh, pltpu.roll(c, sh, axis=1), 0.0)
        sh *= 2

    @pl.when(i % tiles_per_seq == 0)
    def _():
        carry_ref[...] = jnp.zeros_like(carry_ref)

    c = c + jnp.tile(carry_ref[...], (1, tm // LANES))
    c_ref[...] = c
    carry_ref[...] = jnp.broadcast_to(c[:, tm - 1:tm], carry_ref.shape)


def _fox_proj(x2, seq, g, w_in, b_f, q_gain, k_gain):
    n, d = x2.shape
    tm = min(TOKEN_TILE, seq)
    nh = N_HEADS
    assert d == nh * HEAD_DIM and seq % tm == 0
    wq = w_in[:, :d].astype(BF16)
    wkt = w_in[:, d:2 * d].T.astype(BF16)
    wv = w_in[:, 2 * d:3 * d].astype(BF16)
    wft = w_in[:, 3 * d:].T.astype(BF16)
    bfb = jnp.broadcast_to(b_f[:, None], (nh, tm)).astype(F32)
    head = jnp.arange(d) // HEAD_DIM
    e = (head[:, None] == head[None, :]).astype(BF16) * (1.0 / HEAD_DIM)
    qg = (jnp.tile(q_gain, nh) * (LOG2E / math.sqrt(HEAD_DIM)))[None, :]
    kg = jnp.broadcast_to(jnp.tile(k_gain, nh)[:, None], (d, LANES)).astype(F32)
    kern = functools.partial(_fox_proj_kernel, tm=tm, d=d, tiles_per_seq=seq // tm)
    return pl.pallas_call(
        kern,
        out_shape=(
            jax.ShapeDtypeStruct((n, d), BF16),
            jax.ShapeDtypeStruct((d, n), BF16),
            jax.ShapeDtypeStruct((n, d), BF16),
            jax.ShapeDtypeStruct((nh, n), F32),
        ),
        grid=(n // tm,),
        in_specs=[
            pl.BlockSpec((tm, d), lambda i: (i, 0)),
            _const_spec((1, d)),
            _const_spec((d, d)),
            _const_spec((d, d)),
            _const_spec((d, d)),
            _const_spec((nh, d)),
            _const_spec((nh, tm)),
            _const_spec((d, d)),
            _const_spec((1, d)),
            _const_spec((d, LANES)),
        ],
        out_specs=(
            pl.BlockSpec((tm, d), lambda i: (i, 0)),
            pl.BlockSpec((d, tm), lambda i: (0, i)),
            pl.BlockSpec((tm, d), lambda i: (i, 0)),
            pl.BlockSpec((nh, tm), lambda i: (0, i)),
        ),
        scratch_shapes=[pltpu.VMEM((nh, LANES), F32)],
        compiler_params=_params(),
        name="fox_proj",
    )(x2, g[None, :], wq, wkt, wv, wft, bfb, e, qg, kg)


def _split3(a):
    hi = a.astype(BF16).astype(F32)
    r = a - hi
    mid = r.astype(BF16).astype(F32)
    lo = (r - mid).astype(BF16).astype(F32)
    return hi, mid, lo


def _attn_kernel(bound_ref, q_ref, kt_ref, v_ref, c_ref, o_ref, krhs_ref, vext_ref, accx_ref, m_ref, l_ref, acc_ref,
                 cend_ref, *, t, nblk, nhead):
    g = pl.program_id(1)
    i = pl.program_id(2)
    kdim = 2 * LANES
    npair = nhead // 2

    @pl.when(i == 0)
    def _():
        brow = lax.broadcasted_iota(jnp.int32, (BIAS_ROWS, t), 0)
        for e in range(nhead):
            crow = c_ref[pl.ds(nhead * g + e, 1), :] * LOG2E
            hi, mid, lo = (jnp.broadcast_to(a, (BIAS_ROWS, a.shape[1])) for a in _split3(-crow))
            for jb in range(nblk):
                cols = slice(jb * t, (jb + 1) * t)
                cend_ref[e, jb] = crow[0, (jb + 1) * t - 1]
                for eo in range(2):
                    rows = slice(eo * HEAD_DIM, (eo + 1) * HEAD_DIM)
                    if eo == e % 2:
                        src = slice((e // 2) * LANES + eo * HEAD_DIM, (e // 2) * LANES + (eo + 1) * HEAD_DIM)
                        krhs_ref[e, jb, rows, :] = kt_ref[src, cols]
                    else:
                        krhs_ref[e, jb, rows, :] = jnp.zeros((HEAD_DIM, t), BF16)
                bias = jnp.where(brow == 0, hi[:, cols], jnp.where(brow == 1, mid[:, cols],
                                 jnp.where(brow == 2, lo[:, cols], 0.0)))
                krhs_ref[e, jb, LANES:LANES + BIAS_ROWS, :] = bias.astype(BF16)
                krhs_ref[e, jb, LANES + BIAS_ROWS:kdim, :] = jnp.zeros((kdim - LANES - BIAS_ROWS, t), BF16)

        for pr in range(npair):
            vext_ref[pr, :, 0:LANES] = v_ref[:, pr * LANES:(pr + 1) * LANES]
            vext_ref[pr, :, LANES:2 * LANES] = jnp.ones((nblk * t, LANES), BF16)

    ones = jnp.ones((t, LANES), BF16)
    lhs = [jnp.concatenate([q_ref[:, pr * LANES:(pr + 1) * LANES], ones], axis=1) for pr in range(npair)]
    lane = lax.broadcasted_iota(jnp.int32, (t, LANES), 1)

    def causal(s):
        r = lax.broadcasted_iota(jnp.int32, (t, t), 0)
        cc = lax.broadcasted_iota(jnp.int32, (t, t), 1)
        return jnp.where(cc <= r, s, MASK_VALUE)

    for e in range(nhead):
        pr = e // 2
        s = causal(jnp.dot(lhs[pr], krhs_ref[e, i], preferred_element_type=F32))
        m = jnp.max(s, axis=1, keepdims=True)
        m_ref[e] = jnp.broadcast_to(m, (t, LANES))
        p = jnp.exp2(s - m).astype(BF16)
        accx_ref[e] = jnp.dot(p, vext_ref[pr, pl.ds(pl.multiple_of(i * t, t), t), :], preferred_element_type=F32)

    def fast(j, _):
        off = pl.multiple_of(j * t, t)
        for e in range(nhead):
            pr = e // 2
            s = jnp.dot(lhs[pr], krhs_ref[e, j], preferred_element_type=F32)
            p = jnp.exp2(s - jnp.tile(m_ref[e], (1, t // LANES))).astype(BF16)
            accx_ref[e] += jnp.dot(p, vext_ref[pr, pl.ds(off, t), :], preferred_element_type=F32)
        return 0

    start = i
    for e in range(nhead):
        reach = jnp.min(m_ref[e]) - bound_ref[0] - UNDERFLOW_EXP2
        dead = sum((-cend_ref[e, jb] < reach).astype(jnp.int32) for jb in range(nblk))
        start = jnp.minimum(start, dead)
    lax.fori_loop(start, i, fast, 0)
    lmax = jnp.max(accx_ref[:, :, LANES:2 * LANES])
    finite = lmax < FINITE_LIMIT

    @pl.when(finite)
    def _():
        outs = [jnp.where(lane < HEAD_DIM, accx_ref[2 * pr, :, 0:LANES] / accx_ref[2 * pr, :, LANES:2 * LANES],
                          accx_ref[2 * pr + 1, :, 0:LANES] / accx_ref[2 * pr + 1, :, LANES:2 * LANES])
                for pr in range(npair)]
        o_ref[...] = jnp.concatenate(outs, axis=1).astype(BF16)

    @pl.when(jnp.logical_not(finite))
    def _():
        m_ref[...] = jnp.full(m_ref.shape, MASK_VALUE, F32)
        l_ref[...] = jnp.zeros(l_ref.shape, F32)
        acc_ref[...] = jnp.zeros(acc_ref.shape, F32)

        def step(j, masked):
            off = pl.multiple_of(j * t, t)
            for e in range(nhead):
                pr = e // 2
                s = jnp.dot(lhs[pr], krhs_ref[e, j], preferred_element_type=F32)
                if masked:
                    s = causal(s)
                m_prev = m_ref[e]
                m_new = jnp.maximum(m_prev, jnp.max(s, axis=1, keepdims=True))
                alpha = jnp.exp2(m_prev - m_new)
                p = jnp.exp2(s - jnp.tile(m_new, (1, t // LANES)))
                l_ref[e] = alpha * l_ref[e] + jnp.sum(p, axis=1, keepdims=True)
                vblk = v_ref[pl.ds(off, t), pr * LANES:(pr + 1) * LANES]
                acc_ref[e] = alpha * acc_ref[e] + jnp.dot(p.astype(BF16), vblk, preferred_element_type=F32)
                m_ref[e] = m_new

        def body(j, _):
            step(j, False)
            return 0

        lax.fori_loop(0, i, body, 0)
        step(i, True)
        outs = [jnp.where(lane < HEAD_DIM, acc_ref[2 * pr] / l_ref[2 * pr], acc_ref[2 * pr + 1] / l_ref[2 * pr + 1])
                for pr in range(npair)]
        o_ref[...] = jnp.concatenate(outs, axis=1).astype(BF16)


def _attention(bound, q, kt, v, c, bsz, seq):
    n, d = q.shape
    t = min(ATTN_TILE, seq)
    nblk = seq // t
    nhead = ATTN_HEADS
    w = nhead * HEAD_DIM
    assert seq % t == 0 and 2 * HEAD_DIM == LANES and d % w == 0 and w % LANES == 0
    kern = functools.partial(_attn_kernel, t=t, nblk=nblk, nhead=nhead)
    return pl.pallas_call(
        kern,
        out_shape=jax.ShapeDtypeStruct((n, d), BF16),
        grid=(bsz, d // w, nblk),
        in_specs=[
            pl.BlockSpec(memory_space=pltpu.SMEM),
            pl.BlockSpec((t, w), lambda b, g, i: (b * nblk + i, g)),
            pl.BlockSpec((w, seq), lambda b, g, i: (g, b)),
            pl.BlockSpec((seq, w), lambda b, g, i: (b, g)),
            pl.BlockSpec((N_HEADS, seq), lambda b, g, i: (0, b)),
        ],
        out_specs=pl.BlockSpec((t, w), lambda b, g, i: (b * nblk + i, g)),
        scratch_shapes=[
            pltpu.VMEM((nhead, nblk, 2 * LANES, t), BF16),
            pltpu.VMEM((nhead // 2, seq, 2 * LANES), BF16),
            pltpu.VMEM((nhead, t, 2 * LANES), F32),
            pltpu.VMEM((nhead, t, LANES), F32),
            pltpu.VMEM((nhead, t, LANES), F32),
            pltpu.VMEM((nhead, t, LANES), F32),
            pltpu.SMEM((nhead, nblk), F32),
        ],
        compiler_params=pltpu.CompilerParams(dimension_semantics=("arbitrary",) * 3, vmem_limit_bytes=VMEM_LIMIT),
        name="fox_attention",
    )(bound, q, kt, v, c)


def _out_proj_kernel(x_ref, o_ref, w_ref, y_ref):
    y_ref[...] = x_ref[...] + jnp.dot(o_ref[...], w_ref[...], preferred_element_type=F32)


def _out_proj(x2, o, w_o):
    n, d = x2.shape
    tm = min(TOKEN_TILE, n)
    return pl.pallas_call(
        _out_proj_kernel,
        out_shape=jax.ShapeDtypeStruct((n, d), F32),
        grid=(n // tm,),
        in_specs=[
            pl.BlockSpec((tm, d), lambda i: (i, 0)),
            pl.BlockSpec((tm, d), lambda i: (i, 0)),
            _const_spec((d, d)),
        ],
        out_specs=pl.BlockSpec((tm, d), lambda i: (i, 0)),
        compiler_params=_params(),
        name="fox_out_proj",
    )(x2, o, w_o.astype(BF16))


def _fox(x2, bsz, seq, g, w_in, b_f, q_gain, k_gain, w_o):
    q, kt, v, c = _fox_proj(x2, seq, g, w_in, b_f, q_gain, k_gain)
    bound = (QK_BOUND_SLACK * HEAD_DIM * LOG2E / math.sqrt(HEAD_DIM)) * jnp.max(jnp.abs(q_gain)) * jnp.max(
        jnp.abs(k_gain))
    o = _attention(bound.reshape(1).astype(F32), q, kt, v, c, bsz, seq)
    return _out_proj(x2, o, w_o)


def kernel(x, norm_mix, norm_ffn, conv_w_in, conv_b_in, conv_dw, conv_dw_b, conv_ln_g, conv_ln_b, conv_w_out,
           conv_b_out, pool_w, pool_b, pool_scale, fox_w_in, fox_b_f, fox_q_gain, fox_k_gain, fox_w_o, ffn_w_up,
           ffn_dw, ffn_dw_b, ffn_w_down):
    bsz, seq, d = x.shape
    depth = norm_mix.shape[0]
    x2 = x.reshape(bsz * seq, d)
    for i in range(depth):
        j = i // 3
        kind = i % 3
        if kind == 0:
            x2 = _conformer(x2, seq, norm_mix[i], conv_w_in[j], conv_b_in[j], conv_dw[j], conv_dw_b[j],
                            conv_ln_g[j], conv_ln_b[j], conv_w_out[j], conv_b_out[j])
        elif kind == 1:
            x2 = _pool(x2, seq, norm_mix[i], pool_w[j], pool_b[j], pool_scale[j])
        else:
            x2 = _fox(x2, bsz, seq, norm_mix[i], fox_w_in[j], fox_b_f[j], fox_q_gain[j], fox_k_gain[j], fox_w_o[j])
        x2 = _ffn(x2, seq, norm_ffn[i], ffn_w_up[i], ffn_dw[i], ffn_dw_b[i], ffn_w_down[i])
    return x2.reshape(bsz, seq, d)
```

```python
import functools
import math

import jax
import jax.numpy as jnp
from jax import lax
from jax.experimental import pallas as pl
from jax.experimental.pallas import tpu as pltpu

F32 = jnp.float32
BF16 = jnp.bfloat16

EPS = 1e-6
N_HEADS = 16
HEAD_DIM = 64
CONV_WIDTH = 31
FFN_CONV_WIDTH = 3
POOL_WINDOWS = (2, 4, 8, 16)
LANES = 128
SUBLANES = 8
MASK_VALUE = -1e30
FINITE_LIMIT = 3e38
UNDERFLOW_EXP2 = 160.0
QK_BOUND_SLACK = 1.02
BIAS_ROWS = 16

TOKEN_TILE = 512
ATTN_TILE = 512
ATTN_HEADS = 4
LOG2E = math.log2(math.e)
FFN_CHUNK = 256
FFN_TILES_PER_STEP = 2
CONV_ROWS = 128
CONV_HALO = 32
CONF_TILES_PER_STEP = 2
PROJ_CHUNK = 256
VMEM_LIMIT = 56 * 1024 * 1024


def _const_spec(shape):
    nd = len(shape)
    return pl.BlockSpec(shape, lambda *_: (0,) * nd, pipeline_mode=pl.Buffered(1))


def _params():
    return pltpu.CompilerParams(dimension_semantics=("arbitrary",), vmem_limit_bytes=VMEM_LIMIT)


def _rms(x, g):
    ms = jnp.mean(x * x, axis=-1, keepdims=True)
    return x * lax.rsqrt(ms + EPS) * g


def _silu(x):
    return x * jax.nn.sigmoid(x)


def _ffn_kernel(x_ref, g_ref, wup_ref, dwp_ref, wdn_ref, o_ref, hb_ref, ua_ref, ub_ref, carry_ref, *, tm, nsub, nc,
                fc, steps_per_seq):
    i = pl.program_id(0)
    nslab = fc // LANES
    top = SUBLANES
    nwork = nsub * nc

    @pl.when(i % steps_per_seq == 0)
    def _():
        carry_ref[...] = jnp.zeros_like(carry_ref)

    def norm(sub):
        rows = slice(sub * tm, (sub + 1) * tm)
        x = x_ref[rows, :]
        hb_ref[rows, :] = _rms(x, g_ref[...]).astype(BF16)
        o_ref[rows, :] = x

    nf = nc * fc

    def split(w):
        return (w // nc) * tm, w % nc

    def up(w, u_ref):
        r0, c = split(w)
        h = hb_ref[pl.ds(r0, tm), :]
        for p in range(2):
            col = p * nf + c * fc
            u = jnp.dot(h, wup_ref[:, col:col + fc], preferred_element_type=F32)
            for s in range(nslab):
                u_ref[p, s, top:top + tm, :] = u[:, s * LANES:(s + 1) * LANES]

    def gate_down(w, u_ref):
        r0, c = split(w)
        acts = []
        for s in range(nslab):
            ys = []
            for p in range(2):
                u_ref[p, s, 0:top, :] = carry_ref[p, c, s]
                carry_ref[p, c, s] = u_ref[p, s, tm:tm + top, :]
                col = p * nf + c * fc + s * LANES
                t = dwp_ref[:, col:col + LANES]
                y = (t[2:3] * u_ref[p, s, top:top + tm, :] + t[1:2] * u_ref[p, s, top - 1:top - 1 + tm, :]
                     + t[0:1] * u_ref[p, s, top - 2:top - 2 + tm, :]) + t[3:4]
                ys.append(y)
            val, gate = ys
            acts.append((_silu(gate) * val).astype(BF16))
        a = jnp.concatenate(acts, axis=-1)
        o_ref[pl.ds(r0, tm), :] += jnp.dot(a, wdn_ref[c * fc:(c + 1) * fc, :], preferred_element_type=F32)

    bufs = (ua_ref, ub_ref)
    norm(0)
    up(0, bufs[0])
    for sub in range(1, nsub):
        norm(sub)
    for w in range(nwork):
        if w + 1 < nwork:
            up(w + 1, bufs[(w + 1) % 2])
        gate_down(w, bufs[w % 2])


def _ffn(x2, seq, g, w_up, dw, dw_b, w_down):
    n, d = x2.shape
    f = w_down.shape[0]
    fc = FFN_CHUNK
    nc = f // fc
    tm = min(TOKEN_TILE, seq)
    assert f % fc == 0 and seq % tm == 0 and dw.shape[0] == FFN_CONV_WIDTH
    wup = w_up.astype(BF16)
    dwp = jnp.concatenate([dw, dw_b[None, :], jnp.zeros((SUBLANES - 4, 2 * f), F32)], axis=0)
    wdn = w_down.astype(BF16)
    nslab = fc // LANES
    nsub = min(FFN_TILES_PER_STEP, seq // tm)
    ts = nsub * tm
    assert seq % ts == 0
    ubuf = pltpu.VMEM((2, nslab, SUBLANES + tm, LANES), F32)
    kern = functools.partial(_ffn_kernel, tm=tm, nsub=nsub, nc=nc, fc=fc, steps_per_seq=seq // ts)
    return pl.pallas_call(
        kern,
        out_shape=jax.ShapeDtypeStruct((n, d), F32),
        grid=(n // ts,),
        in_specs=[
            pl.BlockSpec((ts, d), lambda i: (i, 0)),
            _const_spec((1, d)),
            _const_spec((d, 2 * f)),
            _const_spec((SUBLANES, 2 * f)),
            _const_spec((f, d)),
        ],
        out_specs=pl.BlockSpec((ts, d), lambda i: (i, 0)),
        scratch_shapes=[
            pltpu.VMEM((ts, d), BF16),
            ubuf,
            ubuf,
            pltpu.VMEM((2, nc, nslab, SUBLANES, LANES), F32),
        ],
        compiler_params=_params(),
        name="conv_ffn",
    )(x2, g[None, :], wup, dwp, wdn)


def _conf_kernel(x_ref, g_ref, win_ref, bin_ref, dws_ref, lng_ref, lnb_ref, wout_ref, bout_ref, o_ref, hb_ref,
                 vbuf_ref, ybuf_ref, pbuf_ref, sw_ref, *, tm, nsub, d, steps_per_seq):
    i = pl.program_id(0)
    halo = CONV_HALO
    nslab = d // LANES
    nch = 2 * d // PROJ_CHUNK
    rb = CONV_ROWS
    first = halo - (CONV_WIDTH - 1)

    @pl.when(i % steps_per_seq == 0)
    def _():
        vbuf_ref[:, 0:halo, :] = jnp.zeros((nslab, halo, LANES), F32)

    @pl.when(i % steps_per_seq != 0)
    def _():
        vbuf_ref[:, 0:halo, :] = vbuf_ref[:, nsub * tm:nsub * tm + halo, :]

    def rows(sub):
        return slice(sub * tm, (sub + 1) * tm)

    def norm(sub):
        hb_ref[rows(sub), :] = _rms(x_ref[rows(sub), :], g_ref[...]).astype(BF16)

    def in_chunk(sub, c):
        pbuf_ref[c] = jnp.dot(hb_ref[rows(sub), :], win_ref[c], preferred_element_type=F32) + bin_ref[c]

    def glu(sub):
        half = PROJ_CHUNK // LANES
        for s in range(nslab):
            lanes = slice((s % half) * LANES, (s % half + 1) * LANES)
            a = pbuf_ref[s // half, :, lanes]
            gate = pbuf_ref[nch // 2 + s // half, :, lanes]
            vbuf_ref[s, halo + sub * tm:halo + (sub + 1) * tm, :] = a * jax.nn.sigmoid(gate)

    def conv_slab(sub, j):
        for r in range(tm // rb):
            base = sub * tm + r * rb
            acc = jnp.broadcast_to(dws_ref[j, CONV_WIDTH:CONV_WIDTH + 1, :], (rb, LANES))
            for k in range(CONV_WIDTH):
                acc = acc + dws_ref[j, k:k + 1, :] * vbuf_ref[j, base + first + k:base + first + k + rb, :]
            ybuf_ref[j, base:base + rb, :] = acc

    def layer_norm_swish(sub):
        y = jnp.concatenate([ybuf_ref[j, rows(sub), :] for j in range(nslab)], axis=-1)
        mu = jnp.mean(y, axis=-1, keepdims=True)
        yc = y - mu
        var = jnp.mean(yc * yc, axis=-1, keepdims=True)
        yn = yc * lax.rsqrt(var + EPS) * lng_ref[...] + lnb_ref[...]
        sw_ref[...] = _silu(yn).astype(BF16)

    def out_chunk(sub, c):
        cols = slice(c * PROJ_CHUNK, (c + 1) * PROJ_CHUNK)
        o_ref[rows(sub), cols] = (x_ref[rows(sub), cols] + bout_ref[:, cols]
                                  + jnp.dot(sw_ref[...], wout_ref[:, cols], preferred_element_type=F32))

    nout = d // PROJ_CHUNK
    for sub in range(nsub):
        norm(sub)
    for c in range(nch):
        in_chunk(0, c)
    glu(0)
    for sub in range(nsub):
        if sub > 0:
            layer_norm_swish(sub - 1)
        for j in range(nslab):
            conv_slab(sub, j)
            if sub + 1 < nsub:
                in_chunk(sub + 1, j)
            if sub > 0 and j % (nslab // nout) == 0:
                out_chunk(sub - 1, j // (nslab // nout))
        if sub + 1 < nsub:
            glu(sub + 1)
    layer_norm_swish(nsub - 1)
    for c in range(nout):
        out_chunk(nsub - 1, c)


def _conformer(x2, seq, g, w_in, b_in, dw, dw_b, ln_g, ln_b, w_out, b_out):
    n, d = x2.shape
    tm = min(TOKEN_TILE, seq)
    nsub = min(CONF_TILES_PER_STEP, seq // tm)
    ts = nsub * tm
    nslab = d // LANES
    nch = 2 * d // PROJ_CHUNK
    assert dw.shape[0] == CONV_WIDTH and seq % ts == 0 and tm % CONV_ROWS == 0 and nch == nslab
    taps = jnp.concatenate([dw, dw_b[None, :]], axis=0)
    dws = taps.reshape(CONV_WIDTH + 1, nslab, LANES).transpose(1, 0, 2)
    win = w_in.astype(BF16).reshape(d, nch, PROJ_CHUNK).transpose(1, 0, 2)
    bin_ = b_in.reshape(nch, 1, PROJ_CHUNK)
    kern = functools.partial(_conf_kernel, tm=tm, nsub=nsub, d=d, steps_per_seq=seq // ts)
    row = lambda a: a[None, :]
    return pl.pallas_call(
        kern,
        out_shape=jax.ShapeDtypeStruct((n, d), F32),
        grid=(n // ts,),
        in_specs=[
            pl.BlockSpec((ts, d), lambda i: (i, 0)),
            _const_spec((1, d)),
            _const_spec((nch, d, PROJ_CHUNK)),
            _const_spec((nch, 1, PROJ_CHUNK)),
            _const_spec((nslab, CONV_WIDTH + 1, LANES)),
            _const_spec((1, d)),
            _const_spec((1, d)),
            _const_spec((d, d)),
            _const_spec((1, d)),
        ],
        out_specs=pl.BlockSpec((ts, d), lambda i: (i, 0)),
        scratch_shapes=[
            pltpu.VMEM((ts, d), BF16),
            pltpu.VMEM((nslab, CONV_HALO + ts, LANES), F32),
            pltpu.VMEM((nslab, ts, LANES), F32),
            pltpu.VMEM((nch, tm, PROJ_CHUNK), F32),
            pltpu.VMEM((tm, d), BF16),
        ],
        compiler_params=_params(),
        name="conformer_conv",
    )(x2, row(g), win, bin_, dws, row(ln_g), row(ln_b), w_out.astype(BF16), row(b_out))


def _pool_kernel(x_ref, g_ref, pw_ref, pb_ref, sc_ref, o_ref, hbuf_ref, *, tm, d, tiles_per_seq):
    i = pl.program_id(0)
    halo = 16
    cg = d // len(POOL_WINDOWS)
    x = x_ref[...]
    h = _rms(x, g_ref[...])

    @pl.when(i % tiles_per_seq == 0)
    def _():
        hbuf_ref[0:halo, :] = jnp.zeros((halo, d), F32)

    @pl.when(i % tiles_per_seq != 0)
    def _():
        hbuf_ref[0:halo, :] = hbuf_ref[tm:tm + halo, :]

    hbuf_ref[halo:halo + tm, :] = h
    pos = (i % tiles_per_seq) * tm + lax.broadcasted_iota(jnp.int32, (tm, cg), 0)
    outs = []
    for gi, w in enumerate(POOL_WINDOWS):
        cols = slice(gi * cg, (gi + 1) * cg)
        hg = h[:, cols]
        s = hg
        for dlt in range(1, w):
            s = s + hbuf_ref[halo - dlt:halo - dlt + tm, cols]
        cnt = jnp.minimum(pos + 1, w).astype(F32)
        pg = (s / cnt - hg).astype(BF16)
        yg = jnp.dot(pg, pw_ref[gi], preferred_element_type=F32) + pb_ref[gi]
        outs.append(yg)
    y = jnp.concatenate(outs, axis=-1)
    o_ref[...] = x + y * sc_ref[...]


def _pool(x2, seq, g, pw, pb, scale):
    n, d = x2.shape
    tm = min(TOKEN_TILE, seq)
    ng = len(POOL_WINDOWS)
    cg = d // ng
    assert seq % tm == 0
    kern = functools.partial(_pool_kernel, tm=tm, d=d, tiles_per_seq=seq // tm)
    return pl.pallas_call(
        kern,
        out_shape=jax.ShapeDtypeStruct((n, d), F32),
        grid=(n // tm,),
        in_specs=[
            pl.BlockSpec((tm, d), lambda i: (i, 0)),
            _const_spec((1, d)),
            _const_spec((ng, cg, cg)),
            _const_spec((ng, 1, cg)),
            _const_spec((1, d)),
        ],
        out_specs=pl.BlockSpec((tm, d), lambda i: (i, 0)),
        scratch_shapes=[pltpu.VMEM((16 + tm, d), F32)],
        compiler_params=_params(),
        name="pool_mixer",
    )(x2, g[None, :], pw.astype(BF16), pb[:, None, :], scale[None, :])


def _fox_proj_kernel(x_ref, g_ref, wq_ref, wkt_ref, wv_ref, wft_ref, bf_ref, e_ref, qg_ref, kg_ref, q_ref, kt_ref,
                     v_ref, c_ref, carry_ref, *, tm, d, tiles_per_seq):
    i = pl.program_id(0)
    hb = _rms(x_ref[...], g_ref[...]).astype(BF16)
    nt = (((1,), (1,)), ((), ()))

    q = jnp.dot(hb, wq_ref[...], preferred_element_type=F32)
    qms = jnp.dot((q * q).astype(BF16), e_ref[...], preferred_element_type=F32)
    q_ref[...] = (q * lax.rsqrt(qms + EPS) * qg_ref[...]).astype(BF16)

    kt = lax.dot_general(wkt_ref[...], hb, nt, preferred_element_type=F32)
    kt3 = kt.reshape(N_HEADS, HEAD_DIM, tm)
    kms = jnp.mean(kt3 * kt3, axis=1, keepdims=True)
    kgain = jnp.tile(kg_ref[...], (1, tm // LANES))
    kt_ref[...] = ((kt3 * lax.rsqrt(kms + EPS)).reshape(d, tm) * kgain).astype(BF16)

    v_ref[...] = jnp.dot(hb, wv_ref[...], preferred_element_type=F32).astype(BF16)

    z = lax.dot_general(wft_ref[...], hb, nt, preferred_element_type=F32) + bf_ref[...]
    logf = jnp.minimum(z, 0.0) - jnp.log1p(jnp.exp(-jnp.abs(z)))
    lane = lax.broadcasted_iota(jnp.int32, logf.shape, 1)
    c = logf
    sh = 1
    while sh < tm:
        c = c + jnp.where(lane >= sh, pltpu.roll(c, sh, axis=1), 0.0)
        sh *= 2

    @pl.when(i % tiles_per_seq == 0)
    def _():
        carry_ref[...] = jnp.zeros_like(carry_ref)

    c = c + jnp.tile(carry_ref[...], (1, tm // LANES))
    c_ref[...] = c
    carry_ref[...] = jnp.broadcast_to(c[:, tm - 1:tm], carry_ref.shape)


def _fox_proj(x2, seq, g, w_in, b_f, q_gain, k_gain):
    n, d = x2.shape
    tm = min(TOKEN_TILE, seq)
    nh = N_HEADS
    assert d == nh * HEAD_DIM and seq % tm == 0
    wq = w_in[:, :d].astype(BF16)
    wkt = w_in[:, d:2 * d].T.astype(BF16)
    wv = w_in[:, 2 * d:3 * d].astype(BF16)
    wft = w_in[:, 3 * d:].T.astype(BF16)
    bfb = jnp.broadcast_to(b_f[:, None], (nh, tm)).astype(F32)
    head = jnp.arange(d) // HEAD_DIM
    e = (head[:, None] == head[None, :]).astype(BF16) * (1.0 / HEAD_DIM)
    qg = (jnp.tile(q_gain, nh) * (LOG2E / math.sqrt(HEAD_DIM)))[None, :]
    kg = jnp.broadcast_to(jnp.tile(k_gain, nh)[:, None], (d, LANES)).astype(F32)
    kern = functools.partial(_fox_proj_kernel, tm=tm, d=d, tiles_per_seq=seq // tm)
    return pl.pallas_call(
        kern,
        out_shape=(
            jax.ShapeDtypeStruct((n, d), BF16),
            jax.ShapeDtypeStruct((d, n), BF16),
            jax.ShapeDtypeStruct((n, d), BF16),
            jax.ShapeDtypeStruct((nh, n), F32),
        ),
        grid=(n // tm,),
        in_specs=[
            pl.BlockSpec((tm, d), lambda i: (i, 0)),
            _const_spec((1, d)),
            _const_spec((d, d)),
            _const_spec((d, d)),
            _const_spec((d, d)),
            _const_spec((nh, d)),
            _const_spec((nh, tm)),
            _const_spec((d, d)),
            _const_spec((1, d)),
            _const_spec((d, LANES)),
        ],
        out_specs=(
            pl.BlockSpec((tm, d), lambda i: (i, 0)),
            pl.BlockSpec((d, tm), lambda i: (0, i)),
            pl.BlockSpec((tm, d), lambda i: (i, 0)),
            pl.BlockSpec((nh, tm), lambda i: (0, i)),
        ),
        scratch_shapes=[pltpu.VMEM((nh, LANES), F32)],
        compiler_params=_params(),
        name="fox_proj",
    )(x2, g[None, :], wq, wkt, wv, wft, bfb, e, qg, kg)


def _split3(a):
    hi = a.astype(BF16).astype(F32)
    r = a - hi
    mid = r.astype(BF16).astype(F32)
    lo = (r - mid).astype(BF16).astype(F32)
    return hi, mid, lo


def _attn_kernel(bound_ref, q_ref, kt_ref, v_ref, c_ref, o_ref, krhs_ref, vext_ref, accx_ref, m_ref, l_ref, acc_ref,
                 cend_ref, *, t, nblk, nhead):
    g = pl.program_id(1)
    i = pl.program_id(2)
    kdim = 2 * LANES
    npair = nhead // 2

    @pl.when(i == 0)
    def _():
        brow = lax.broadcasted_iota(jnp.int32, (BIAS_ROWS, t), 0)
        for e in range(nhead):
            crow = c_ref[pl.ds(nhead * g + e, 1), :] * LOG2E
            hi, mid, lo = (jnp.broadcast_to(a, (BIAS_ROWS, a.shape[1])) for a in _split3(-crow))
            for jb in range(nblk):
                cols = slice(jb * t, (jb + 1) * t)
                cend_ref[e, jb] = crow[0, (jb + 1) * t - 1]
                for eo in range(2):
                    rows = slice(eo * HEAD_DIM, (eo + 1) * HEAD_DIM)
                    if eo == e % 2:
                        src = slice((e // 2) * LANES + eo * HEAD_DIM, (e // 2) * LANES + (eo + 1) * HEAD_DIM)
                        krhs_ref[e, jb, rows, :] = kt_ref[src, cols]
                    else:
                        krhs_ref[e, jb, rows, :] = jnp.zeros((HEAD_DIM, t), BF16)
                bias = jnp.where(brow == 0, hi[:, cols], jnp.where(brow == 1, mid[:, cols],
                                 jnp.where(brow == 2, lo[:, cols], 0.0)))
                krhs_ref[e, jb, LANES:LANES + BIAS_ROWS, :] = bias.astype(BF16)
                krhs_ref[e, jb, LANES + BIAS_ROWS:kdim, :] = jnp.zeros((kdim - LANES - BIAS_ROWS, t), BF16)

        for pr in range(npair):
            vext_ref[pr, :, 0:LANES] = v_ref[:, pr * LANES:(pr + 1) * LANES]
            vext_ref[pr, :, LANES:2 * LANES] = jnp.ones((nblk * t, LANES), BF16)

    ones = jnp.ones((t, LANES), BF16)
    lhs = [jnp.concatenate([q_ref[:, pr * LANES:(pr + 1) * LANES], ones], axis=1) for pr in range(npair)]
    lane = lax.broadcasted_iota(jnp.int32, (t, LANES), 1)

    def causal(s):
        r = lax.broadcasted_iota(jnp.int32, (t, t), 0)
        cc = lax.broadcasted_iota(jnp.int32, (t, t), 1)
        return jnp.where(cc <= r, s, MASK_VALUE)

    hq = t // 2
    tri_r = lax.broadcasted_iota(jnp.int32, (hq, hq), 0)
    tri_c = lax.broadcasted_iota(jnp.int32, (hq, hq), 1)
    own = pl.multiple_of(i * t, t)
    for e in range(nhead):
        pr = e // 2
        s_top = jnp.dot(lhs[pr][0:hq], krhs_ref[e, i, :, 0:hq], preferred_element_type=F32)
        s_top = jnp.where(tri_c <= tri_r, s_top, MASK_VALUE)
        m_top = jnp.max(s_top, axis=1, keepdims=True)
        p_top = jnp.exp2(s_top - m_top).astype(BF16)
        accx_ref[e, 0:hq, :] = jnp.dot(p_top, vext_ref[pr, pl.ds(own, hq), :], preferred_element_type=F32)
        m_ref[e, 0:hq, :] = jnp.broadcast_to(m_top, (hq, LANES))

        s_bot = jnp.dot(lhs[pr][hq:t], krhs_ref[e, i], preferred_element_type=F32)
        s_left = s_bot[:, 0:hq]
        s_right = jnp.where(tri_c <= tri_r, s_bot[:, hq:t], MASK_VALUE)
        m_bot = jnp.maximum(jnp.max(s_left, axis=1, keepdims=True), jnp.max(s_right, axis=1, keepdims=True))
        p_bot = jnp.concatenate([jnp.exp2(s_left - m_bot), jnp.exp2(s_right - m_bot)], axis=1).astype(BF16)
        accx_ref[e, hq:t, :] = jnp.dot(p_bot, vext_ref[pr, pl.ds(own, t), :], preferred_element_type=F32)
        m_ref[e, hq:t, :] = jnp.broadcast_to(m_bot, (hq, LANES))

    def fast(j, _):
        off = pl.multiple_of(j * t, t)
        for e in range(nhead):
            pr = e // 2
            s = jnp.dot(lhs[pr], krhs_ref[e, j], preferred_element_type=F32)
            p = jnp.exp2(s - jnp.tile(m_ref[e], (1, t // LANES))).astype(BF16)
            accx_ref[e] += jnp.dot(p, vext_ref[pr, pl.ds(off, t), :], preferred_element_type=F32)
        return 0

    start = i
    for e in range(nhead):
        reach = jnp.min(m_ref[e]) - bound_ref[0] - UNDERFLOW_EXP2
        dead = sum((-cend_ref[e, jb] < reach).astype(jnp.int32) for jb in range(nblk))
        start = jnp.minimum(start, dead)
    lax.fori_loop(start, i, fast, 0)
    lmax = jnp.max(accx_ref[:, :, LANES:2 * LANES])
    finite = lmax < FINITE_LIMIT

    @pl.when(finite)
    def _():
        outs = [jnp.where(lane < HEAD_DIM, accx_ref[2 * pr, :, 0:LANES] / accx_ref[2 * pr, :, LANES:2 * LANES],
                          accx_ref[2 * pr + 1, :, 0:LANES] / accx_ref[2 * pr + 1, :, LANES:2 * LANES])
                for pr in range(npair)]
        o_ref[...] = jnp.concatenate(outs, axis=1).astype(BF16)

    @pl.when(jnp.logical_not(finite))
    def _():
        m_ref[...] = jnp.full(m_ref.shape, MASK_VALUE, F32)
        l_ref[...] = jnp.zeros(l_ref.shape, F32)
        acc_ref[...] = jnp.zeros(acc_ref.shape, F32)

        def step(j, masked):
            off = pl.multiple_of(j * t, t)
            for e in range(nhead):
                pr = e // 2
                s = jnp.dot(lhs[pr], krhs_ref[e, j], preferred_element_type=F32)
                if masked:
                    s = causal(s)
                m_prev = m_ref[e]
                m_new = jnp.maximum(m_prev, jnp.max(s, axis=1, keepdims=True))
                alpha = jnp.exp2(m_prev - m_new)
                p = jnp.exp2(s - jnp.tile(m_new, (1, t // LANES)))
                l_ref[e] = alpha * l_ref[e] + jnp.sum(p, axis=1, keepdims=True)
                vblk = v_ref[pl.ds(off, t), pr * LANES:(pr + 1) * LANES]
                acc_ref[e] = alpha * acc_ref[e] + jnp.dot(p.astype(BF16), vblk, preferred_element_type=F32)
                m_ref[e] = m_new

        def body(j, _):
            step(j, False)
            return 0

        lax.fori_loop(0, i, body, 0)
        step(i, True)
        outs = [jnp.where(lane < HEAD_DIM, acc_ref[2 * pr] / l_ref[2 * pr], acc_ref[2 * pr + 1] / l_ref[2 * pr + 1])
                for pr in range(npair)]
        o_ref[...] = jnp.concatenate(outs, axis=1).astype(BF16)


def _attention(bound, q, kt, v, c, bsz, seq):
    n, d = q.shape
    t = min(ATTN_TILE, seq)
    nblk = seq // t
    nhead = ATTN_HEADS
    w = nhead * HEAD_DIM
    assert seq % t == 0 and 2 * HEAD_DIM == LANES and d % w == 0 and w % LANES == 0
    kern = functools.partial(_attn_kernel, t=t, nblk=nblk, nhead=nhead)
    return pl.pallas_call(
        kern,
        out_shape=jax.ShapeDtypeStruct((n, d), BF16),
        grid=(bsz, d // w, nblk),
        in_specs=[
            pl.BlockSpec(memory_space=pltpu.SMEM),
            pl.BlockSpec((t, w), lambda b, g, i: (b * nblk + i, g)),
            pl.BlockSpec((w, seq), lambda b, g, i: (g, b)),
            pl.BlockSpec((seq, w), lambda b, g, i: (b, g)),
            pl.BlockSpec((N_HEADS, seq), lambda b, g, i: (0, b)),
        ],
        out_specs=pl.BlockSpec((t, w), lambda b, g, i: (b * nblk + i, g)),
        scratch_shapes=[
            pltpu.VMEM((nhead, nblk, 2 * LANES, t), BF16),
            pltpu.VMEM((nhead // 2, seq, 2 * LANES), BF16),
            pltpu.VMEM((nhead, t, 2 * LANES), F32),
            pltpu.VMEM((nhead, t, LANES), F32),
            pltpu.VMEM((nhead, t, LANES), F32),
            pltpu.VMEM((nhead, t, LANES), F32),
            pltpu.SMEM((nhead, nblk), F32),
        ],
        compiler_params=pltpu.CompilerParams(dimension_semantics=("arbitrary",) * 3, vmem_limit_bytes=VMEM_LIMIT),
        name="fox_attention",
    )(bound, q, kt, v, c)


def _out_proj_kernel(x_ref, o_ref, w_ref, y_ref):
    y_ref[...] = x_ref[...] + jnp.dot(o_ref[...], w_ref[...], preferred_element_type=F32)


def _out_proj(x2, o, w_o):
    n, d = x2.shape
    tm = min(TOKEN_TILE, n)
    return pl.pallas_call(
        _out_proj_kernel,
        out_shape=jax.ShapeDtypeStruct((n, d), F32),
        grid=(n // tm,),
        in_specs=[
            pl.BlockSpec((tm, d), lambda i: (i, 0)),
            pl.BlockSpec((tm, d), lambda i: (i, 0)),
            _const_spec((d, d)),
        ],
        out_specs=pl.BlockSpec((tm, d), lambda i: (i, 0)),
        compiler_params=_params(),
        name="fox_out_proj",
    )(x2, o, w_o.astype(BF16))


def _fox(x2, bsz, seq, g, w_in, b_f, q_gain, k_gain, w_o):
    q, kt, v, c = _fox_proj(x2, seq, g, w_in, b_f, q_gain, k_gain)
    bound = (QK_BOUND_SLACK * HEAD_DIM * LOG2E / math.sqrt(HEAD_DIM)) * jnp.max(jnp.abs(q_gain)) * jnp.max(
        jnp.abs(k_gain))
    o = _attention(bound.reshape(1).astype(F32), q, kt, v, c, bsz, seq)
    return _out_proj(x2, o, w_o)


def kernel(x, norm_mix, norm_ffn, conv_w_in, conv_b_in, conv_dw, conv_dw_b, conv_ln_g, conv_ln_b, conv_w_out,
           conv_b_out, pool_w, pool_b, pool_scale, fox_w_in, fox_b_f, fox_q_gain, fox_k_gain, fox_w_o, ffn_w_up,
           ffn_dw, ffn_dw_b, ffn_w_down):
    bsz, seq, d = x.shape
    depth = norm_mix.shape[0]
    x2 = x.reshape(bsz * seq, d)
    for i in range(depth):
        j = i // 3
        kind = i % 3
        if kind == 0:
            x2 = _conformer(x2, seq, norm_mix[i], conv_w_in[j], conv_b_in[j], conv_dw[j], conv_dw_b[j],
                            conv_ln_g[j], conv_ln_b[j], conv_w_out[j], conv_b_out[j])
        elif kind == 1:
            x2 = _pool(x2, seq, norm_mix[i], pool_w[j], pool_b[j], pool_scale[j])
        else:
            x2 = _fox(x2, bsz, seq, norm_mix[i], fox_w_in[j], fox_b_f[j], fox_q_gain[j], fox_k_gain[j], fox_w_o[j])
        x2 = _ffn(x2, seq, norm_ffn[i], ffn_w_up[i], ffn_dw[i], ffn_dw_b[i], ffn_w_down[i])
    return x2.reshape(bsz, seq, d)
```

```python
import functools
import math

import jax
import jax.numpy as jnp
from jax import lax
from jax.experimental import pallas as pl
from jax.experimental.pallas import tpu as pltpu

F32 = jnp.float32
BF16 = jnp.bfloat16

EPS = 1e-6
N_HEADS = 16
HEAD_DIM = 64
CONV_WIDTH = 31
FFN_CONV_WIDTH = 3
POOL_WINDOWS = (2, 4, 8, 16)
LANES = 128
SUBLANES = 8
MASK_VALUE = -1e30
FINITE_LIMIT = 3e38
UNDERFLOW_EXP2 = 160.0
QK_BOUND_SLACK = 1.02
BIAS_ROWS = 16

TOKEN_TILE = 512
ATTN_TILE = 512
ATTN_HEADS = 4
LOG2E = math.log2(math.e)
FFN_CHUNK = 256
FFN_TILES_PER_STEP = 2
CONV_ROWS = 128
CONV_HALO = 32
CONF_TILES_PER_STEP = 2
PROJ_TILES_PER_STEP = 2
PROJ_CHUNK = 256
VMEM_LIMIT = 56 * 1024 * 1024


def _const_spec(shape):
    nd = len(shape)
    return pl.BlockSpec(shape, lambda *_: (0,) * nd, pipeline_mode=pl.Buffered(1))


def _params():
    return pltpu.CompilerParams(dimension_semantics=("arbitrary",), vmem_limit_bytes=VMEM_LIMIT)


def _rms(x, g):
    ms = jnp.mean(x * x, axis=-1, keepdims=True)
    return x * lax.rsqrt(ms + EPS) * g


def _silu(x):
    return x * jax.nn.sigmoid(x)


def _ffn_kernel(x_ref, g_ref, wup_ref, dwp_ref, wdn_ref, o_ref, hb_ref, ua_ref, ub_ref, carry_ref, *, tm, nsub, nc,
                fc, steps_per_seq):
    i = pl.program_id(0)
    nslab = fc // LANES
    top = SUBLANES
    nwork = nsub * nc

    @pl.when(i % steps_per_seq == 0)
    def _():
        carry_ref[...] = jnp.zeros_like(carry_ref)

    def norm(sub):
        rows = slice(sub * tm, (sub + 1) * tm)
        x = x_ref[rows, :]
        hb_ref[rows, :] = _rms(x, g_ref[...]).astype(BF16)
        o_ref[rows, :] = x

    nf = nc * fc

    def split(w):
        return (w // nc) * tm, w % nc

    def up(w, u_ref):
        r0, c = split(w)
        h = hb_ref[pl.ds(r0, tm), :]
        for p in range(2):
            col = p * nf + c * fc
            u = jnp.dot(h, wup_ref[:, col:col + fc], preferred_element_type=F32)
            for s in range(nslab):
                u_ref[p, s, top:top + tm, :] = u[:, s * LANES:(s + 1) * LANES]

    def gate_down(w, u_ref):
        r0, c = split(w)
        acts = []
        for s in range(nslab):
            ys = []
            for p in range(2):
                u_ref[p, s, 0:top, :] = carry_ref[p, c, s]
                carry_ref[p, c, s] = u_ref[p, s, tm:tm + top, :]
                col = p * nf + c * fc + s * LANES
                t = dwp_ref[:, col:col + LANES]
                y = (t[2:3] * u_ref[p, s, top:top + tm, :] + t[1:2] * u_ref[p, s, top - 1:top - 1 + tm, :]
                     + t[0:1] * u_ref[p, s, top - 2:top - 2 + tm, :]) + t[3:4]
                ys.append(y)
            val, gate = ys
            acts.append((_silu(gate) * val).astype(BF16))
        a = jnp.concatenate(acts, axis=-1)
        o_ref[pl.ds(r0, tm), :] += jnp.dot(a, wdn_ref[c * fc:(c + 1) * fc, :], preferred_element_type=F32)

    bufs = (ua_ref, ub_ref)
    norm(0)
    up(0, bufs[0])
    for sub in range(1, nsub):
        norm(sub)
    for w in range(nwork):
        if w + 1 < nwork:
            up(w + 1, bufs[(w + 1) % 2])
        gate_down(w, bufs[w % 2])


def _ffn(x2, seq, g, w_up, dw, dw_b, w_down):
    n, d = x2.shape
    f = w_down.shape[0]
    fc = FFN_CHUNK
    nc = f // fc
    tm = min(TOKEN_TILE, seq)
    assert f % fc == 0 and seq % tm == 0 and dw.shape[0] == FFN_CONV_WIDTH
    wup = w_up.astype(BF16)
    dwp = jnp.concatenate([dw, dw_b[None, :], jnp.zeros((SUBLANES - 4, 2 * f), F32)], axis=0)
    wdn = w_down.astype(BF16)
    nslab = fc // LANES
    nsub = min(FFN_TILES_PER_STEP, seq // tm)
    ts = nsub * tm
    assert seq % ts == 0
    ubuf = pltpu.VMEM((2, nslab, SUBLANES + tm, LANES), F32)
    kern = functools.partial(_ffn_kernel, tm=tm, nsub=nsub, nc=nc, fc=fc, steps_per_seq=seq // ts)
    return pl.pallas_call(
        kern,
        out_shape=jax.ShapeDtypeStruct((n, d), F32),
        grid=(n // ts,),
        in_specs=[
            pl.BlockSpec((ts, d), lambda i: (i, 0)),
            _const_spec((1, d)),
            _const_spec((d, 2 * f)),
            _const_spec((SUBLANES, 2 * f)),
            _const_spec((f, d)),
        ],
        out_specs=pl.BlockSpec((ts, d), lambda i: (i, 0)),
        scratch_shapes=[
            pltpu.VMEM((ts, d), BF16),
            ubuf,
            ubuf,
            pltpu.VMEM((2, nc, nslab, SUBLANES, LANES), F32),
        ],
        compiler_params=_params(),
        name="conv_ffn",
    )(x2, g[None, :], wup, dwp, wdn)


def _conf_kernel(x_ref, g_ref, win_ref, bin_ref, dws_ref, lng_ref, lnb_ref, wout_ref, bout_ref, o_ref, hb_ref,
                 vbuf_ref, ybuf_ref, pbuf_ref, sw_ref, *, tm, nsub, d, steps_per_seq):
    i = pl.program_id(0)
    halo = CONV_HALO
    nslab = d // LANES
    nch = 2 * d // PROJ_CHUNK
    rb = CONV_ROWS
    first = halo - (CONV_WIDTH - 1)

    @pl.when(i % steps_per_seq == 0)
    def _():
        vbuf_ref[:, 0:halo, :] = jnp.zeros((nslab, halo, LANES), F32)

    @pl.when(i % steps_per_seq != 0)
    def _():
        vbuf_ref[:, 0:halo, :] = vbuf_ref[:, nsub * tm:nsub * tm + halo, :]

    def rows(sub):
        return slice(sub * tm, (sub + 1) * tm)

    def norm(sub):
        hb_ref[rows(sub), :] = _rms(x_ref[rows(sub), :], g_ref[...]).astype(BF16)

    def in_chunk(sub, c):
        pbuf_ref[c] = jnp.dot(hb_ref[rows(sub), :], win_ref[c], preferred_element_type=F32) + bin_ref[c]

    def glu(sub):
        half = PROJ_CHUNK // LANES
        for s in range(nslab):
            lanes = slice((s % half) * LANES, (s % half + 1) * LANES)
            a = pbuf_ref[s // half, :, lanes]
            gate = pbuf_ref[nch // 2 + s // half, :, lanes]
            vbuf_ref[s, halo + sub * tm:halo + (sub + 1) * tm, :] = a * jax.nn.sigmoid(gate)

    def conv_slab(sub, j):
        for r in range(tm // rb):
            base = sub * tm + r * rb
            acc = jnp.broadcast_to(dws_ref[j, CONV_WIDTH:CONV_WIDTH + 1, :], (rb, LANES))
            for k in range(CONV_WIDTH):
                acc = acc + dws_ref[j, k:k + 1, :] * vbuf_ref[j, base + first + k:base + first + k + rb, :]
            ybuf_ref[j, base:base + rb, :] = acc

    def layer_norm_swish(sub):
        y = jnp.concatenate([ybuf_ref[j, rows(sub), :] for j in range(nslab)], axis=-1)
        mu = jnp.mean(y, axis=-1, keepdims=True)
        yc = y - mu
        var = jnp.mean(yc * yc, axis=-1, keepdims=True)
        yn = yc * lax.rsqrt(var + EPS) * lng_ref[...] + lnb_ref[...]
        sw_ref[...] = _silu(yn).astype(BF16)

    def out_chunk(sub, c):
        cols = slice(c * PROJ_CHUNK, (c + 1) * PROJ_CHUNK)
        o_ref[rows(sub), cols] = (x_ref[rows(sub), cols] + bout_ref[:, cols]
                                  + jnp.dot(sw_ref[...], wout_ref[:, cols], preferred_element_type=F32))

    nout = d // PROJ_CHUNK
    for sub in range(nsub):
        norm(sub)
    for c in range(nch):
        in_chunk(0, c)
    glu(0)
    for sub in range(nsub):
        if sub > 0:
            layer_norm_swish(sub - 1)
        for j in range(nslab):
            conv_slab(sub, j)
            if sub + 1 < nsub:
                in_chunk(sub + 1, j)
            if sub > 0 and j % (nslab // nout) == 0:
                out_chunk(sub - 1, j // (nslab // nout))
        if sub + 1 < nsub:
            glu(sub + 1)
    layer_norm_swish(nsub - 1)
    for c in range(nout):
        out_chunk(nsub - 1, c)


def _conformer(x2, seq, g, w_in, b_in, dw, dw_b, ln_g, ln_b, w_out, b_out):
    n, d = x2.shape
    tm = min(TOKEN_TILE, seq)
    nsub = min(CONF_TILES_PER_STEP, seq // tm)
    ts = nsub * tm
    nslab = d // LANES
    nch = 2 * d // PROJ_CHUNK
    assert dw.shape[0] == CONV_WIDTH and seq % ts == 0 and tm % CONV_ROWS == 0 and nch == nslab
    taps = jnp.concatenate([dw, dw_b[None, :]], axis=0)
    dws = taps.reshape(CONV_WIDTH + 1, nslab, LANES).transpose(1, 0, 2)
    win = w_in.astype(BF16).reshape(d, nch, PROJ_CHUNK).transpose(1, 0, 2)
    bin_ = b_in.reshape(nch, 1, PROJ_CHUNK)
    kern = functools.partial(_conf_kernel, tm=tm, nsub=nsub, d=d, steps_per_seq=seq // ts)
    row = lambda a: a[None, :]
    return pl.pallas_call(
        kern,
        out_shape=jax.ShapeDtypeStruct((n, d), F32),
        grid=(n // ts,),
        in_specs=[
            pl.BlockSpec((ts, d), lambda i: (i, 0)),
            _const_spec((1, d)),
            _const_spec((nch, d, PROJ_CHUNK)),
            _const_spec((nch, 1, PROJ_CHUNK)),
            _const_spec((nslab, CONV_WIDTH + 1, LANES)),
            _const_spec((1, d)),
            _const_spec((1, d)),
            _const_spec((d, d)),
            _const_spec((1, d)),
        ],
        out_specs=pl.BlockSpec((ts, d), lambda i: (i, 0)),
        scratch_shapes=[
            pltpu.VMEM((ts, d), BF16),
            pltpu.VMEM((nslab, CONV_HALO + ts, LANES), F32),
            pltpu.VMEM((nslab, ts, LANES), F32),
            pltpu.VMEM((nch, tm, PROJ_CHUNK), F32),
            pltpu.VMEM((tm, d), BF16),
        ],
        compiler_params=_params(),
        name="conformer_conv",
    )(x2, row(g), win, bin_, dws, row(ln_g), row(ln_b), w_out.astype(BF16), row(b_out))


def _pool_kernel(x_ref, g_ref, pw_ref, pb_ref, sc_ref, o_ref, hbuf_ref, *, tm, d, tiles_per_seq):
    i = pl.program_id(0)
    halo = 16
    cg = d // len(POOL_WINDOWS)
    x = x_ref[...]
    h = _rms(x, g_ref[...])

    @pl.when(i % tiles_per_seq == 0)
    def _():
        hbuf_ref[0:halo, :] = jnp.zeros((halo, d), F32)

    @pl.when(i % tiles_per_seq != 0)
    def _():
        hbuf_ref[0:halo, :] = hbuf_ref[tm:tm + halo, :]

    hbuf_ref[halo:halo + tm, :] = h
    pos = (i % tiles_per_seq) * tm + lax.broadcasted_iota(jnp.int32, (tm, cg), 0)
    outs = []
    for gi, w in enumerate(POOL_WINDOWS):
        cols = slice(gi * cg, (gi + 1) * cg)
        hg = h[:, cols]
        s = hg
        for dlt in range(1, w):
            s = s + hbuf_ref[halo - dlt:halo - dlt + tm, cols]
        cnt = jnp.minimum(pos + 1, w).astype(F32)
        pg = (s / cnt - hg).astype(BF16)
        yg = jnp.dot(pg, pw_ref[gi], preferred_element_type=F32) + pb_ref[gi]
        outs.append(yg)
    y = jnp.concatenate(outs, axis=-1)
    o_ref[...] = x + y * sc_ref[...]


def _pool(x2, seq, g, pw, pb, scale):
    n, d = x2.shape
    tm = min(TOKEN_TILE, seq)
    ng = len(POOL_WINDOWS)
    cg = d // ng
    assert seq % tm == 0
    kern = functools.partial(_pool_kernel, tm=tm, d=d, tiles_per_seq=seq // tm)
    return pl.pallas_call(
        kern,
        out_shape=jax.ShapeDtypeStruct((n, d), F32),
        grid=(n // tm,),
        in_specs=[
            pl.BlockSpec((tm, d), lambda i: (i, 0)),
            _const_spec((1, d)),
            _const_spec((ng, cg, cg)),
            _const_spec((ng, 1, cg)),
            _const_spec((1, d)),
        ],
        out_specs=pl.BlockSpec((tm, d), lambda i: (i, 0)),
        scratch_shapes=[pltpu.VMEM((16 + tm, d), F32)],
        compiler_params=_params(),
        name="pool_mixer",
    )(x2, g[None, :], pw.astype(BF16), pb[:, None, :], scale[None, :])


def _fox_proj_kernel(x_ref, g_ref, wq_ref, wkt_ref, wv_ref, wft_ref, bf_ref, e_ref, qg_ref, kg_ref, q_ref, kt_ref,
                     v_ref, c_ref, carry_ref, *, tm, nsub, d, steps_per_seq):
    i = pl.program_id(0)
    nt = (((1,), (1,)), ((), ()))

    @pl.when(i % steps_per_seq == 0)
    def _():
        carry_ref[...] = jnp.zeros_like(carry_ref)

    for sub in range(nsub):
        rows = slice(sub * tm, (sub + 1) * tm)
        hb = _rms(x_ref[rows, :], g_ref[...]).astype(BF16)

        q = jnp.dot(hb, wq_ref[...], preferred_element_type=F32)
        qms = jnp.dot((q * q).astype(BF16), e_ref[...], preferred_element_type=F32)
        q_ref[rows, :] = (q * lax.rsqrt(qms + EPS) * qg_ref[...]).astype(BF16)

        kt = lax.dot_general(wkt_ref[...], hb, nt, preferred_element_type=F32)
        kt3 = kt.reshape(N_HEADS, HEAD_DIM, tm)
        kms = jnp.mean(kt3 * kt3, axis=1, keepdims=True)
        kgain = jnp.tile(kg_ref[...], (1, tm // LANES))
        kt_ref[:, rows] = ((kt3 * lax.rsqrt(kms + EPS)).reshape(d, tm) * kgain).astype(BF16)

        v_ref[rows, :] = jnp.dot(hb, wv_ref[...], preferred_element_type=F32).astype(BF16)

        z = lax.dot_general(wft_ref[...], hb, nt, preferred_element_type=F32) + bf_ref[...]
        logf = jnp.minimum(z, 0.0) - jnp.log1p(jnp.exp(-jnp.abs(z)))
        lane = lax.broadcasted_iota(jnp.int32, logf.shape, 1)
        c = logf
        sh = 1
        while sh < tm:
            c = c + jnp.where(lane >= sh, pltpu.roll(c, sh, axis=1), 0.0)
            sh *= 2
        c = c + jnp.tile(carry_ref[...], (1, tm // LANES))
        c_ref[:, rows] = c
        carry_ref[...] = jnp.broadcast_to(c[:, tm - 1:tm], carry_ref.shape)


def _fox_proj(x2, seq, g, w_in, b_f, q_gain, k_gain):
    n, d = x2.shape
    tm = min(TOKEN_TILE, seq)
    nsub = min(PROJ_TILES_PER_STEP, seq // tm)
    ts = nsub * tm
    nh = N_HEADS
    assert d == nh * HEAD_DIM and seq % ts == 0
    wq = w_in[:, :d].astype(BF16)
    wkt = w_in[:, d:2 * d].T.astype(BF16)
    wv = w_in[:, 2 * d:3 * d].astype(BF16)
    wft = w_in[:, 3 * d:].T.astype(BF16)
    bfb = jnp.broadcast_to(b_f[:, None], (nh, tm)).astype(F32)
    head = jnp.arange(d) // HEAD_DIM
    e = (head[:, None] == head[None, :]).astype(BF16) * (1.0 / HEAD_DIM)
    qg = (jnp.tile(q_gain, nh) * (LOG2E / math.sqrt(HEAD_DIM)))[None, :]
    kg = jnp.broadcast_to(jnp.tile(k_gain, nh)[:, None], (d, LANES)).astype(F32)
    kern = functools.partial(_fox_proj_kernel, tm=tm, nsub=nsub, d=d, steps_per_seq=seq // ts)
    return pl.pallas_call(
        kern,
        out_shape=(
            jax.ShapeDtypeStruct((n, d), BF16),
            jax.ShapeDtypeStruct((d, n), BF16),
            jax.ShapeDtypeStruct((n, d), BF16),
            jax.ShapeDtypeStruct((nh, n), F32),
        ),
        grid=(n // ts,),
        in_specs=[
            pl.BlockSpec((ts, d), lambda i: (i, 0)),
            _const_spec((1, d)),
            _const_spec((d, d)),
            _const_spec((d, d)),
            _const_spec((d, d)),
            _const_spec((nh, d)),
            _const_spec((nh, tm)),
            _const_spec((d, d)),
            _const_spec((1, d)),
            _const_spec((d, LANES)),
        ],
        out_specs=(
            pl.BlockSpec((ts, d), lambda i: (i, 0)),
            pl.BlockSpec((d, ts), lambda i: (0, i)),
            pl.BlockSpec((ts, d), lambda i: (i, 0)),
            pl.BlockSpec((nh, ts), lambda i: (0, i)),
        ),
        scratch_shapes=[pltpu.VMEM((nh, LANES), F32)],
        compiler_params=_params(),
        name="fox_proj",
    )(x2, g[None, :], wq, wkt, wv, wft, bfb, e, qg, kg)


def _split3(a):
    hi = a.astype(BF16).astype(F32)
    r = a - hi
    mid = r.astype(BF16).astype(F32)
    lo = (r - mid).astype(BF16).astype(F32)
    return hi, mid, lo


def _attn_kernel(bound_ref, q_ref, kt_ref, v_ref, c_ref, o_ref, krhs_ref, vext_ref, accx_ref, m_ref, l_ref, acc_ref,
                 cend_ref, *, t, nblk, nhead):
    g = pl.program_id(1)
    i = pl.program_id(2)
    kdim = 2 * LANES
    npair = nhead // 2

    @pl.when(i == 0)
    def _():
        brow = lax.broadcasted_iota(jnp.int32, (BIAS_ROWS, t), 0)
        for e in range(nhead):
            crow = c_ref[pl.ds(nhead * g + e, 1), :] * LOG2E
            hi, mid, lo = (jnp.broadcast_to(a, (BIAS_ROWS, a.shape[1])) for a in _split3(-crow))
            for jb in range(nblk):
                cols = slice(jb * t, (jb + 1) * t)
                cend_ref[e, jb] = crow[0, (jb + 1) * t - 1]
                for eo in range(2):
                    rows = slice(eo * HEAD_DIM, (eo + 1) * HEAD_DIM)
                    if eo == e % 2:
                        src = slice((e // 2) * LANES + eo * HEAD_DIM, (e // 2) * LANES + (eo + 1) * HEAD_DIM)
                        krhs_ref[e, jb, rows, :] = kt_ref[src, cols]
                    else:
                        krhs_ref[e, jb, rows, :] = jnp.zeros((HEAD_DIM, t), BF16)
                bias = jnp.where(brow == 0, hi[:, cols], jnp.where(brow == 1, mid[:, cols],
                                 jnp.where(brow == 2, lo[:, cols], 0.0)))
                krhs_ref[e, jb, LANES:LANES + BIAS_ROWS, :] = bias.astype(BF16)
                krhs_ref[e, jb, LANES + BIAS_ROWS:kdim, :] = jnp.zeros((kdim - LANES - BIAS_ROWS, t), BF16)

        for pr in range(npair):
            vext_ref[pr, :, 0:LANES] = v_ref[:, pr * LANES:(pr + 1) * LANES]
            vext_ref[pr, :, LANES:2 * LANES] = jnp.ones((nblk * t, LANES), BF16)

    ones = jnp.ones((t, LANES), BF16)
    lhs = [jnp.concatenate([q_ref[:, pr * LANES:(pr + 1) * LANES], ones], axis=1) for pr in range(npair)]
    lane = lax.broadcasted_iota(jnp.int32, (t, LANES), 1)

    def causal(s):
        r = lax.broadcasted_iota(jnp.int32, (t, t), 0)
        cc = lax.broadcasted_iota(jnp.int32, (t, t), 1)
        return jnp.where(cc <= r, s, MASK_VALUE)

    for e in range(nhead):
        pr = e // 2
        s = causal(jnp.dot(lhs[pr], krhs_ref[e, i], preferred_element_type=F32))
        m = jnp.max(s, axis=1, keepdims=True)
        m_ref[e] = jnp.broadcast_to(m, (t, LANES))
        p = jnp.exp2(s - m).astype(BF16)
        accx_ref[e] = jnp.dot(p, vext_ref[pr, pl.ds(pl.multiple_of(i * t, t), t), :], preferred_element_type=F32)

    def fast(j, _):
        off = pl.multiple_of(j * t, t)
        for e in range(nhead):
            pr = e // 2
            s = jnp.dot(lhs[pr], krhs_ref[e, j], preferred_element_type=F32)
            p = jnp.exp2(s - jnp.tile(m_ref[e], (1, t // LANES))).astype(BF16)
            accx_ref[e] += jnp.dot(p, vext_ref[pr, pl.ds(off, t), :], preferred_element_type=F32)
        return 0

    start = i
    for e in range(nhead):
        reach = jnp.min(m_ref[e]) - bound_ref[0] - UNDERFLOW_EXP2
        dead = sum((-cend_ref[e, jb] < reach).astype(jnp.int32) for jb in range(nblk))
        start = jnp.minimum(start, dead)
    lax.fori_loop(start, i, fast, 0)
    lmax = jnp.max(accx_ref[:, :, LANES:2 * LANES])
    finite = lmax < FINITE_LIMIT

    @pl.when(finite)
    def _():
        outs = [jnp.where(lane < HEAD_DIM, accx_ref[2 * pr, :, 0:LANES] / accx_ref[2 * pr, :, LANES:2 * LANES],
                          accx_ref[2 * pr + 1, :, 0:LANES] / accx_ref[2 * pr + 1, :, LANES:2 * LANES])
                for pr in range(npair)]
        o_ref[...] = jnp.concatenate(outs, axis=1).astype(BF16)

    @pl.when(jnp.logical_not(finite))
    def _():
        m_ref[...] = jnp.full(m_ref.shape, MASK_VALUE, F32)
        l_ref[...] = jnp.zeros(l_ref.shape, F32)
        acc_ref[...] = jnp.zeros(acc_ref.shape, F32)

        def step(j, masked):
            off = pl.multiple_of(j * t, t)
            for e in range(nhead):
                pr = e // 2
                s = jnp.dot(lhs[pr], krhs_ref[e, j], preferred_element_type=F32)
                if masked:
                    s = causal(s)
                m_prev = m_ref[e]
                m_new = jnp.maximum(m_prev, jnp.max(s, axis=1, keepdims=True))
                alpha = jnp.exp2(m_prev - m_new)
                p = jnp.exp2(s - jnp.tile(m_new, (1, t // LANES)))
                l_ref[e] = alpha * l_ref[e] + jnp.sum(p, axis=1, keepdims=True)
                vblk = v_ref[pl.ds(off, t), pr * LANES:(pr + 1) * LANES]
                acc_ref[e] = alpha * acc_ref[e] + jnp.dot(p.astype(BF16), vblk, preferred_element_type=F32)
                m_ref[e] = m_new

        def body(j, _):
            step(j, False)
            return 0

        lax.fori_loop(0, i, body, 0)
        step(i, True)
        outs = [jnp.where(lane < HEAD_DIM, acc_ref[2 * pr] / l_ref[2 * pr], acc_ref[2 * pr + 1] / l_ref[2 * pr + 1])
                for pr in range(npair)]
        o_ref[...] = jnp.concatenate(outs, axis=1).astype(BF16)


def _attention(bound, q, kt, v, c, bsz, seq):
    n, d = q.shape
    t = min(ATTN_TILE, seq)
    nblk = seq // t
    nhead = ATTN_HEADS
    w = nhead * HEAD_DIM
    assert seq % t == 0 and 2 * HEAD_DIM == LANES and d % w == 0 and w % LANES == 0
    kern = functools.partial(_attn_kernel, t=t, nblk=nblk, nhead=nhead)
    return pl.pallas_call(
        kern,
        out_shape=jax.ShapeDtypeStruct((n, d), BF16),
        grid=(bsz, d // w, nblk),
        in_specs=[
            pl.BlockSpec(memory_space=pltpu.SMEM),
            pl.BlockSpec((t, w), lambda b, g, i: (b * nblk + i, g)),
            pl.BlockSpec((w, seq), lambda b, g, i: (g, b)),
            pl.BlockSpec((seq, w), lambda b, g, i: (b, g)),
            pl.BlockSpec((N_HEADS, seq), lambda b, g, i: (0, b)),
        ],
        out_specs=pl.BlockSpec((t, w), lambda b, g, i: (b * nblk + i, g)),
        scratch_shapes=[
            pltpu.VMEM((nhead, nblk, 2 * LANES, t), BF16),
            pltpu.VMEM((nhead // 2, seq, 2 * LANES), BF16),
            pltpu.VMEM((nhead, t, 2 * LANES), F32),
            pltpu.VMEM((nhead, t, LANES), F32),
            pltpu.VMEM((nhead, t, LANES), F32),
            pltpu.VMEM((nhead, t, LANES), F32),
            pltpu.SMEM((nhead, nblk), F32),
        ],
        compiler_params=pltpu.CompilerParams(dimension_semantics=("arbitrary",) * 3, vmem_limit_bytes=VMEM_LIMIT),
        name="fox_attention",
    )(bound, q, kt, v, c)


def _out_proj_kernel(x_ref, o_ref, w_ref, y_ref):
    y_ref[...] = x_ref[...] + jnp.dot(o_ref[...], w_ref[...], preferred_element_type=F32)


def _out_proj(x2, o, w_o):
    n, d = x2.shape
    tm = min(TOKEN_TILE, n)
    return pl.pallas_call(
        _out_proj_kernel,
        out_shape=jax.ShapeDtypeStruct((n, d), F32),
        grid=(n // tm,),
        in_specs=[
            pl.BlockSpec((tm, d), lambda i: (i, 0)),
            pl.BlockSpec((tm, d), lambda i: (i, 0)),
            _const_spec((d, d)),
        ],
        out_specs=pl.BlockSpec((tm, d), lambda i: (i, 0)),
        compiler_params=_params(),
        name="fox_out_proj",
    )(x2, o, w_o.astype(BF16))


def _fox(x2, bsz, seq, g, w_in, b_f, q_gain, k_gain, w_o):
    q, kt, v, c = _fox_proj(x2, seq, g, w_in, b_f, q_gain, k_gain)
    bound = (QK_BOUND_SLACK * HEAD_DIM * LOG2E / math.sqrt(HEAD_DIM)) * jnp.max(jnp.abs(q_gain)) * jnp.max(
        jnp.abs(k_gain))
    o = _attention(bound.reshape(1).astype(F32), q, kt, v, c, bsz, seq)
    return _out_proj(x2, o, w_o)


def kernel(x, norm_mix, norm_ffn, conv_w_in, conv_b_in, conv_dw, conv_dw_b, conv_ln_g, conv_ln_b, conv_w_out,
           conv_b_out, pool_w, pool_b, pool_scale, fox_w_in, fox_b_f, fox_q_gain, fox_k_gain, fox_w_o, ffn_w_up,
           ffn_dw, ffn_dw_b, ffn_w_down):
    bsz, seq, d = x.shape
    depth = norm_mix.shape[0]
    x2 = x.reshape(bsz * seq, d)
    for i in range(depth):
        j = i // 3
        kind = i % 3
        if kind == 0:
            x2 = _conformer(x2, seq, norm_mix[i], conv_w_in[j], conv_b_in[j], conv_dw[j], conv_dw_b[j],
                            conv_ln_g[j], conv_ln_b[j], conv_w_out[j], conv_b_out[j])
        elif kind == 1:
            x2 = _pool(x2, seq, norm_mix[i], pool_w[j], pool_b[j], pool_scale[j])
        else:
            x2 = _fox(x2, bsz, seq, norm_mix[i], fox_w_in[j], fox_b_f[j], fox_q_gain[j], fox_k_gain[j], fox_w_o[j])
        x2 = _ffn(x2, seq, norm_ffn[i], ffn_w_up[i], ffn_dw[i], ffn_dw_b[i], ffn_w_down[i])
    return x2.reshape(bsz, seq, d)
```

```python
import functools
import math

import jax
import jax.numpy as jnp
from jax import lax
from jax.experimental import pallas as pl
from jax.experimental.pallas import tpu as pltpu

F32 = jnp.float32
BF16 = jnp.bfloat16

EPS = 1e-6
N_HEADS = 16
HEAD_DIM = 64
CONV_WIDTH = 31
FFN_CONV_WIDTH = 3
POOL_WINDOWS = (2, 4, 8, 16)
LANES = 128
SUBLANES = 8
MASK_VALUE = -1e30
FINITE_LIMIT = 3e38
UNDERFLOW_EXP2 = 160.0
QK_BOUND_SLACK = 1.02
BIAS_ROWS = 16

TOKEN_TILE = 512
ATTN_TILE = 512
ATTN_HEADS = 4
LOG2E = math.log2(math.e)
FFN_CHUNK = 256
FFN_TILES_PER_STEP = 2
CONV_ROWS = 128
CONV_HALO = 32
CONF_TILES_PER_STEP = 2
PROJ_TILES_PER_STEP = 2
PROJ_CHUNK = 256
VMEM_LIMIT = 56 * 1024 * 1024


def _const_spec(shape):
    nd = len(shape)
    return pl.BlockSpec(shape, lambda *_: (0,) * nd, pipeline_mode=pl.Buffered(1))


def _params():
    return pltpu.CompilerParams(dimension_semantics=("arbitrary",), vmem_limit_bytes=VMEM_LIMIT)


def _rms(x, g):
    ms = jnp.mean(x * x, axis=-1, keepdims=True)
    return x * lax.rsqrt(ms + EPS) * g


def _silu(x):
    return x * jax.nn.sigmoid(x)


def _ffn_kernel(x_ref, g_ref, wup_ref, dwp_ref, wdn_ref, o_ref, hb_ref, ua_ref, ub_ref, carry_ref, *, tm, nsub, nc,
                fc, steps_per_seq):
    i = pl.program_id(0)
    nslab = fc // LANES
    top = SUBLANES
    nwork = nsub * nc

    @pl.when(i % steps_per_seq == 0)
    def _():
        carry_ref[...] = jnp.zeros_like(carry_ref)

    def norm(sub):
        rows = slice(sub * tm, (sub + 1) * tm)
        hb_ref[rows, :] = _rms(x_ref[rows, :], g_ref[...]).astype(BF16)

    nf = nc * fc

    def split(w):
        return (w // nc) * tm, w % nc

    def up(w, u_ref):
        r0, c = split(w)
        h = hb_ref[pl.ds(r0, tm), :]
        for p in range(2):
            col = p * nf + c * fc
            u = jnp.dot(h, wup_ref[:, col:col + fc], preferred_element_type=F32)
            for s in range(nslab):
                u_ref[p, s, top:top + tm, :] = u[:, s * LANES:(s + 1) * LANES]

    def gate_down(w, u_ref):
        r0, c = split(w)
        acts = []
        for s in range(nslab):
            ys = []
            for p in range(2):
                u_ref[p, s, 0:top, :] = carry_ref[p, c, s]
                carry_ref[p, c, s] = u_ref[p, s, tm:tm + top, :]
                col = p * nf + c * fc + s * LANES
                t = dwp_ref[:, col:col + LANES]
                y = (t[2:3] * u_ref[p, s, top:top + tm, :] + t[1:2] * u_ref[p, s, top - 1:top - 1 + tm, :]
                     + t[0:1] * u_ref[p, s, top - 2:top - 2 + tm, :]) + t[3:4]
                ys.append(y)
            val, gate = ys
            acts.append((_silu(gate) * val).astype(BF16))
        a = jnp.concatenate(acts, axis=-1)
        part = jnp.dot(a, wdn_ref[c * fc:(c + 1) * fc, :], preferred_element_type=F32)
        if c == 0:
            o_ref[pl.ds(r0, tm), :] = x_ref[pl.ds(r0, tm), :] + part
        else:
            o_ref[pl.ds(r0, tm), :] += part

    bufs = (ua_ref, ub_ref)
    norm(0)
    up(0, bufs[0])
    for sub in range(1, nsub):
        norm(sub)
    for w in range(nwork):
        if w + 1 < nwork:
            up(w + 1, bufs[(w + 1) % 2])
        gate_down(w, bufs[w % 2])


def _ffn(x2, seq, g, w_up, dw, dw_b, w_down):
    n, d = x2.shape
    f = w_down.shape[0]
    fc = FFN_CHUNK
    nc = f // fc
    tm = min(TOKEN_TILE, seq)
    assert f % fc == 0 and seq % tm == 0 and dw.shape[0] == FFN_CONV_WIDTH
    wup = w_up.astype(BF16)
    dwp = jnp.concatenate([dw, dw_b[None, :], jnp.zeros((SUBLANES - 4, 2 * f), F32)], axis=0)
    wdn = w_down.astype(BF16)
    nslab = fc // LANES
    nsub = min(FFN_TILES_PER_STEP, seq // tm)
    ts = nsub * tm
    assert seq % ts == 0
    ubuf = pltpu.VMEM((2, nslab, SUBLANES + tm, LANES), F32)
    kern = functools.partial(_ffn_kernel, tm=tm, nsub=nsub, nc=nc, fc=fc, steps_per_seq=seq // ts)
    return pl.pallas_call(
        kern,
        out_shape=jax.ShapeDtypeStruct((n, d), F32),
        grid=(n // ts,),
        in_specs=[
            pl.BlockSpec((ts, d), lambda i: (i, 0)),
            _const_spec((1, d)),
            _const_spec((d, 2 * f)),
            _const_spec((SUBLANES, 2 * f)),
            _const_spec((f, d)),
        ],
        out_specs=pl.BlockSpec((ts, d), lambda i: (i, 0)),
        scratch_shapes=[
            pltpu.VMEM((ts, d), BF16),
            ubuf,
            ubuf,
            pltpu.VMEM((2, nc, nslab, SUBLANES, LANES), F32),
        ],
        compiler_params=_params(),
        name="conv_ffn",
    )(x2, g[None, :], wup, dwp, wdn)


def _conf_kernel(x_ref, g_ref, win_ref, bin_ref, dws_ref, lng_ref, lnb_ref, wout_ref, bout_ref, o_ref, hb_ref,
                 vbuf_ref, ybuf_ref, pbuf_ref, sw_ref, *, tm, nsub, d, steps_per_seq):
    i = pl.program_id(0)
    halo = CONV_HALO
    nslab = d // LANES
    nch = 2 * d // PROJ_CHUNK
    rb = CONV_ROWS
    first = halo - (CONV_WIDTH - 1)

    @pl.when(i % steps_per_seq == 0)
    def _():
        vbuf_ref[:, 0:halo, :] = jnp.zeros((nslab, halo, LANES), F32)

    @pl.when(i % steps_per_seq != 0)
    def _():
        vbuf_ref[:, 0:halo, :] = vbuf_ref[:, nsub * tm:nsub * tm + halo, :]

    def rows(sub):
        return slice(sub * tm, (sub + 1) * tm)

    def norm(sub):
        hb_ref[rows(sub), :] = _rms(x_ref[rows(sub), :], g_ref[...]).astype(BF16)

    def in_chunk(sub, c):
        pbuf_ref[c] = jnp.dot(hb_ref[rows(sub), :], win_ref[c], preferred_element_type=F32) + bin_ref[c]

    def glu(sub):
        half = PROJ_CHUNK // LANES
        for s in range(nslab):
            lanes = slice((s % half) * LANES, (s % half + 1) * LANES)
            a = pbuf_ref[s // half, :, lanes]
            gate = pbuf_ref[nch // 2 + s // half, :, lanes]
            vbuf_ref[s, halo + sub * tm:halo + (sub + 1) * tm, :] = a * jax.nn.sigmoid(gate)

    def conv_slab(sub, j):
        for r in range(tm // rb):
            base = sub * tm + r * rb
            acc = jnp.broadcast_to(dws_ref[j, CONV_WIDTH:CONV_WIDTH + 1, :], (rb, LANES))
            for k in range(CONV_WIDTH):
                acc = acc + dws_ref[j, k:k + 1, :] * vbuf_ref[j, base + first + k:base + first + k + rb, :]
            ybuf_ref[j, base:base + rb, :] = acc

    def layer_norm_swish(sub):
        y = jnp.concatenate([ybuf_ref[j, rows(sub), :] for j in range(nslab)], axis=-1)
        mu = jnp.mean(y, axis=-1, keepdims=True)
        yc = y - mu
        var = jnp.mean(yc * yc, axis=-1, keepdims=True)
        yn = yc * lax.rsqrt(var + EPS) * lng_ref[...] + lnb_ref[...]
        sw_ref[...] = _silu(yn).astype(BF16)

    def out_chunk(sub, c):
        cols = slice(c * PROJ_CHUNK, (c + 1) * PROJ_CHUNK)
        o_ref[rows(sub), cols] = (x_ref[rows(sub), cols] + bout_ref[:, cols]
                                  + jnp.dot(sw_ref[...], wout_ref[:, cols], preferred_element_type=F32))

    nout = d // PROJ_CHUNK
    for sub in range(nsub):
        norm(sub)
    for c in range(nch):
        in_chunk(0, c)
    glu(0)
    for sub in range(nsub):
        if sub > 0:
            layer_norm_swish(sub - 1)
        for j in range(nslab):
            conv_slab(sub, j)
            if sub + 1 < nsub:
                in_chunk(sub + 1, j)
            if sub > 0 and j % (nslab // nout) == 0:
                out_chunk(sub - 1, j // (nslab // nout))
        if sub + 1 < nsub:
            glu(sub + 1)
    layer_norm_swish(nsub - 1)
    for c in range(nout):
        out_chunk(nsub - 1, c)


def _conformer(x2, seq, g, w_in, b_in, dw, dw_b, ln_g, ln_b, w_out, b_out):
    n, d = x2.shape
    tm = min(TOKEN_TILE, seq)
    nsub = min(CONF_TILES_PER_STEP, seq // tm)
    ts = nsub * tm
    nslab = d // LANES
    nch = 2 * d // PROJ_CHUNK
    assert dw.shape[0] == CONV_WIDTH and seq % ts == 0 and tm % CONV_ROWS == 0 and nch == nslab
    taps = jnp.concatenate([dw, dw_b[None, :]], axis=0)
    dws = taps.reshape(CONV_WIDTH + 1, nslab, LANES).transpose(1, 0, 2)
    win = w_in.astype(BF16).reshape(d, nch, PROJ_CHUNK).transpose(1, 0, 2)
    bin_ = b_in.reshape(nch, 1, PROJ_CHUNK)
    kern = functools.partial(_conf_kernel, tm=tm, nsub=nsub, d=d, steps_per_seq=seq // ts)
    row = lambda a: a[None, :]
    return pl.pallas_call(
        kern,
        out_shape=jax.ShapeDtypeStruct((n, d), F32),
        grid=(n // ts,),
        in_specs=[
            pl.BlockSpec((ts, d), lambda i: (i, 0)),
            _const_spec((1, d)),
            _const_spec((nch, d, PROJ_CHUNK)),
            _const_spec((nch, 1, PROJ_CHUNK)),
            _const_spec((nslab, CONV_WIDTH + 1, LANES)),
            _const_spec((1, d)),
            _const_spec((1, d)),
            _const_spec((d, d)),
            _const_spec((1, d)),
        ],
        out_specs=pl.BlockSpec((ts, d), lambda i: (i, 0)),
        scratch_shapes=[
            pltpu.VMEM((ts, d), BF16),
            pltpu.VMEM((nslab, CONV_HALO + ts, LANES), F32),
            pltpu.VMEM((nslab, ts, LANES), F32),
            pltpu.VMEM((nch, tm, PROJ_CHUNK), F32),
            pltpu.VMEM((tm, d), BF16),
        ],
        compiler_params=_params(),
        name="conformer_conv",
    )(x2, row(g), win, bin_, dws, row(ln_g), row(ln_b), w_out.astype(BF16), row(b_out))


def _pool_kernel(x_ref, g_ref, pw_ref, pb_ref, sc_ref, o_ref, hbuf_ref, *, tm, d, tiles_per_seq):
    i = pl.program_id(0)
    halo = 16
    cg = d // len(POOL_WINDOWS)
    x = x_ref[...]
    h = _rms(x, g_ref[...])

    @pl.when(i % tiles_per_seq == 0)
    def _():
        hbuf_ref[0:halo, :] = jnp.zeros((halo, d), F32)

    @pl.when(i % tiles_per_seq != 0)
    def _():
        hbuf_ref[0:halo, :] = hbuf_ref[tm:tm + halo, :]

    hbuf_ref[halo:halo + tm, :] = h
    pos = (i % tiles_per_seq) * tm + lax.broadcasted_iota(jnp.int32, (tm, LANES), 0)
    outs = []
    for gi, w in enumerate(POOL_WINDOWS):
        cols = slice(gi * cg, (gi + 1) * cg)
        hg = h[:, cols]
        s = hg
        for dlt in range(1, w):
            s = s + hbuf_ref[halo - dlt:halo - dlt + tm, cols]
        cnt = jnp.tile(jnp.minimum(pos + 1, w).astype(F32), (1, cg // LANES))
        pg = (s / cnt - hg).astype(BF16)
        yg = jnp.dot(pg, pw_ref[gi], preferred_element_type=F32) + pb_ref[gi]
        outs.append(yg)
    y = jnp.concatenate(outs, axis=-1)
    o_ref[...] = x + y * sc_ref[...]


def _pool(x2, seq, g, pw, pb, scale):
    n, d = x2.shape
    tm = min(TOKEN_TILE, seq)
    ng = len(POOL_WINDOWS)
    cg = d // ng
    assert seq % tm == 0
    kern = functools.partial(_pool_kernel, tm=tm, d=d, tiles_per_seq=seq // tm)
    return pl.pallas_call(
        kern,
        out_shape=jax.ShapeDtypeStruct((n, d), F32),
        grid=(n // tm,),
        in_specs=[
            pl.BlockSpec((tm, d), lambda i: (i, 0)),
            _const_spec((1, d)),
            _const_spec((ng, cg, cg)),
            _const_spec((ng, 1, cg)),
            _const_spec((1, d)),
        ],
        out_specs=pl.BlockSpec((tm, d), lambda i: (i, 0)),
        scratch_shapes=[pltpu.VMEM((16 + tm, d), F32)],
        compiler_params=_params(),
        name="pool_mixer",
    )(x2, g[None, :], pw.astype(BF16), pb[:, None, :], scale[None, :])


def _fox_proj_kernel(x_ref, g_ref, wq_ref, wkt_ref, wv_ref, wft_ref, bf_ref, e_ref, qg_ref, kg_ref, q_ref, kt_ref,
                     v_ref, c_ref, carry_ref, *, tm, nsub, d, steps_per_seq):
    i = pl.program_id(0)
    nt = (((1,), (1,)), ((), ()))

    @pl.when(i % steps_per_seq == 0)
    def _():
        carry_ref[...] = jnp.zeros_like(carry_ref)

    for sub in range(nsub):
        rows = slice(sub * tm, (sub + 1) * tm)
        hb = _rms(x_ref[rows, :], g_ref[...]).astype(BF16)

        q = jnp.dot(hb, wq_ref[...], preferred_element_type=F32)
        qms = jnp.dot((q * q).astype(BF16), e_ref[...], preferred_element_type=F32)
        q_ref[rows, :] = (q * lax.rsqrt(qms + EPS) * qg_ref[...]).astype(BF16)

        kt = lax.dot_general(wkt_ref[...], hb, nt, preferred_element_type=F32)
        kt3 = kt.reshape(N_HEADS, HEAD_DIM, tm)
        kms = jnp.mean(kt3 * kt3, axis=1, keepdims=True)
        kgain = jnp.tile(kg_ref[...], (1, tm // LANES))
        kt_ref[:, rows] = ((kt3 * lax.rsqrt(kms + EPS)).reshape(d, tm) * kgain).astype(BF16)

        v_ref[rows, :] = jnp.dot(hb, wv_ref[...], preferred_element_type=F32).astype(BF16)

        z = lax.dot_general(wft_ref[...], hb, nt, preferred_element_type=F32) + bf_ref[...]
        logf = jnp.minimum(z, 0.0) - jnp.log1p(jnp.exp(-jnp.abs(z)))
        lane = lax.broadcasted_iota(jnp.int32, logf.shape, 1)
        c = logf
        sh = 1
        while sh < tm:
            c = c + jnp.where(lane >= sh, pltpu.roll(c, sh, axis=1), 0.0)
            sh *= 2
        c = c + jnp.tile(carry_ref[...], (1, tm // LANES))
        c_ref[:, rows] = c
        carry_ref[...] = jnp.broadcast_to(c[:, tm - 1:tm], carry_ref.shape)


def _fox_proj(x2, seq, g, w_in, b_f, q_gain, k_gain):
    n, d = x2.shape
    tm = min(TOKEN_TILE, seq)
    nsub = min(PROJ_TILES_PER_STEP, seq // tm)
    ts = nsub * tm
    nh = N_HEADS
    assert d == nh * HEAD_DIM and seq % ts == 0
    wq = w_in[:, :d].astype(BF16)
    wkt = w_in[:, d:2 * d].T.astype(BF16)
    wv = w_in[:, 2 * d:3 * d].astype(BF16)
    wft = w_in[:, 3 * d:].T.astype(BF16)
    bfb = jnp.broadcast_to(b_f[:, None], (nh, tm)).astype(F32)
    head = jnp.arange(d) // HEAD_DIM
    e = (head[:, None] == head[None, :]).astype(BF16) * (1.0 / HEAD_DIM)
    qg = (jnp.tile(q_gain, nh) * (LOG2E / math.sqrt(HEAD_DIM)))[None, :]
    kg = jnp.broadcast_to(jnp.tile(k_gain, nh)[:, None], (d, LANES)).astype(F32)
    kern = functools.partial(_fox_proj_kernel, tm=tm, nsub=nsub, d=d, steps_per_seq=seq // ts)
    return pl.pallas_call(
        kern,
        out_shape=(
            jax.ShapeDtypeStruct((n, d), BF16),
            jax.ShapeDtypeStruct((d, n), BF16),
            jax.ShapeDtypeStruct((n, d), BF16),
            jax.ShapeDtypeStruct((nh, n), F32),
        ),
        grid=(n // ts,),
        in_specs=[
            pl.BlockSpec((ts, d), lambda i: (i, 0)),
            _const_spec((1, d)),
            _const_spec((d, d)),
            _const_spec((d, d)),
            _const_spec((d, d)),
            _const_spec((nh, d)),
            _const_spec((nh, tm)),
            _const_spec((d, d)),
            _const_spec((1, d)),
            _const_spec((d, LANES)),
        ],
        out_specs=(
            pl.BlockSpec((ts, d), lambda i: (i, 0)),
            pl.BlockSpec((d, ts), lambda i: (0, i)),
            pl.BlockSpec((ts, d), lambda i: (i, 0)),
            pl.BlockSpec((nh, ts), lambda i: (0, i)),
        ),
        scratch_shapes=[pltpu.VMEM((nh, LANES), F32)],
        compiler_params=_params(),
        name="fox_proj",
    )(x2, g[None, :], wq, wkt, wv, wft, bfb, e, qg, kg)


def _split3(a):
    hi = a.astype(BF16).astype(F32)
    r = a - hi
    mid = r.astype(BF16).astype(F32)
    lo = (r - mid).astype(BF16).astype(F32)
    return hi, mid, lo


def _attn_kernel(bound_ref, q_ref, kt_ref, v_ref, c_ref, o_ref, krhs_ref, vext_ref, accx_ref, m_ref, l_ref, acc_ref,
                 cend_ref, *, t, nblk, nhead):
    g = pl.program_id(1)
    i = pl.program_id(2)
    kdim = 2 * LANES
    npair = nhead // 2

    @pl.when(i == 0)
    def _():
        brow = lax.broadcasted_iota(jnp.int32, (BIAS_ROWS, t), 0)
        for e in range(nhead):
            crow = c_ref[pl.ds(nhead * g + e, 1), :] * LOG2E
            hi, mid, lo = (jnp.broadcast_to(a, (BIAS_ROWS, a.shape[1])) for a in _split3(-crow))
            for jb in range(nblk):
                cols = slice(jb * t, (jb + 1) * t)
                cend_ref[e, jb] = crow[0, (jb + 1) * t - 1]
                for eo in range(2):
                    rows = slice(eo * HEAD_DIM, (eo + 1) * HEAD_DIM)
                    if eo == e % 2:
                        src = slice((e // 2) * LANES + eo * HEAD_DIM, (e // 2) * LANES + (eo + 1) * HEAD_DIM)
                        krhs_ref[e, jb, rows, :] = kt_ref[src, cols]
                    else:
                        krhs_ref[e, jb, rows, :] = jnp.zeros((HEAD_DIM, t), BF16)
                bias = jnp.where(brow == 0, hi[:, cols], jnp.where(brow == 1, mid[:, cols],
                                 jnp.where(brow == 2, lo[:, cols], 0.0)))
                krhs_ref[e, jb, LANES:LANES + BIAS_ROWS, :] = bias.astype(BF16)
                krhs_ref[e, jb, LANES + BIAS_ROWS:kdim, :] = jnp.zeros((kdim - LANES - BIAS_ROWS, t), BF16)

        for pr in range(npair):
            vext_ref[pr, :, 0:LANES] = v_ref[:, pr * LANES:(pr + 1) * LANES]
            vext_ref[pr, :, LANES:2 * LANES] = jnp.ones((nblk * t, LANES), BF16)

    ones = jnp.ones((t, LANES), BF16)
    lhs = [jnp.concatenate([q_ref[:, pr * LANES:(pr + 1) * LANES], ones], axis=1) for pr in range(npair)]
    lane = lax.broadcasted_iota(jnp.int32, (t, LANES), 1)

    def causal(s):
        r = lax.broadcasted_iota(jnp.int32, (t, t), 0)
        cc = lax.broadcasted_iota(jnp.int32, (t, t), 1)
        return jnp.where(cc <= r, s, MASK_VALUE)

    for e in range(nhead):
        pr = e // 2
        s = causal(jnp.dot(lhs[pr], krhs_ref[e, i], preferred_element_type=F32))
        m = jnp.max(s, axis=1, keepdims=True)
        m_ref[e] = jnp.broadcast_to(m, (t, LANES))
        p = jnp.exp2(s - m).astype(BF16)
        accx_ref[e] = jnp.dot(p, vext_ref[pr, pl.ds(pl.multiple_of(i * t, t), t), :], preferred_element_type=F32)

    def fast(j, _):
        off = pl.multiple_of(j * t, t)
        for e in range(nhead):
            pr = e // 2
            s = jnp.dot(lhs[pr], krhs_ref[e, j], preferred_element_type=F32)
            p = jnp.exp2(s - jnp.tile(m_ref[e], (1, t // LANES))).astype(BF16)
            accx_ref[e] += jnp.dot(p, vext_ref[pr, pl.ds(off, t), :], preferred_element_type=F32)
        return 0

    start = i
    for e in range(nhead):
        reach = jnp.min(m_ref[e]) - bound_ref[0] - UNDERFLOW_EXP2
        dead = sum((-cend_ref[e, jb] < reach).astype(jnp.int32) for jb in range(nblk))
        start = jnp.minimum(start, dead)
    lax.fori_loop(start, i, fast, 0)
    lmax = jnp.max(accx_ref[:, :, LANES:2 * LANES])
    finite = lmax < FINITE_LIMIT

    @pl.when(finite)
    def _():
        outs = [jnp.where(lane < HEAD_DIM, accx_ref[2 * pr, :, 0:LANES] / accx_ref[2 * pr, :, LANES:2 * LANES],
                          accx_ref[2 * pr + 1, :, 0:LANES] / accx_ref[2 * pr + 1, :, LANES:2 * LANES])
                for pr in range(npair)]
        o_ref[...] = jnp.concatenate(outs, axis=1).astype(BF16)

    @pl.when(jnp.logical_not(finite))
    def _():
        m_ref[...] = jnp.full(m_ref.shape, MASK_VALUE, F32)
        l_ref[...] = jnp.zeros(l_ref.shape, F32)
        acc_ref[...] = jnp.zeros(acc_ref.shape, F32)

        def step(j, masked):
            off = pl.multiple_of(j * t, t)
            for e in range(nhead):
                pr = e // 2
                s = jnp.dot(lhs[pr], krhs_ref[e, j], preferred_element_type=F32)
                if masked:
                    s = causal(s)
                m_prev = m_ref[e]
                m_new = jnp.maximum(m_prev, jnp.max(s, axis=1, keepdims=True))
                alpha = jnp.exp2(m_prev - m_new)
                p = jnp.exp2(s - jnp.tile(m_new, (1, t // LANES)))
                l_ref[e] = alpha * l_ref[e] + jnp.sum(p, axis=1, keepdims=True)
                vblk = v_ref[pl.ds(off, t), pr * LANES:(pr + 1) * LANES]
                acc_ref[e] = alpha * acc_ref[e] + jnp.dot(p.astype(BF16), vblk, preferred_element_type=F32)
                m_ref[e] = m_new

        def body(j, _):
            step(j, False)
            return 0

        lax.fori_loop(0, i, body, 0)
        step(i, True)
        outs = [jnp.where(lane < HEAD_DIM, acc_ref[2 * pr] / l_ref[2 * pr], acc_ref[2 * pr + 1] / l_ref[2 * pr + 1])
                for pr in range(npair)]
        o_ref[...] = jnp.concatenate(outs, axis=1).astype(BF16)


def _attention(bound, q, kt, v, c, bsz, seq):
    n, d = q.shape
    t = min(ATTN_TILE, seq)
    nblk = seq // t
    nhead = ATTN_HEADS
    w = nhead * HEAD_DIM
    assert seq % t == 0 and 2 * HEAD_DIM == LANES and d % w == 0 and w % LANES == 0
    kern = functools.partial(_attn_kernel, t=t, nblk=nblk, nhead=nhead)
    return pl.pallas_call(
        kern,
        out_shape=jax.ShapeDtypeStruct((n, d), BF16),
        grid=(bsz, d // w, nblk),
        in_specs=[
            pl.BlockSpec(memory_space=pltpu.SMEM),
            pl.BlockSpec((t, w), lambda b, g, i: (b * nblk + i, g)),
            pl.BlockSpec((w, seq), lambda b, g, i: (g, b)),
            pl.BlockSpec((seq, w), lambda b, g, i: (b, g)),
            pl.BlockSpec((N_HEADS, seq), lambda b, g, i: (0, b)),
        ],
        out_specs=pl.BlockSpec((t, w), lambda b, g, i: (b * nblk + i, g)),
        scratch_shapes=[
            pltpu.VMEM((nhead, nblk, 2 * LANES, t), BF16),
            pltpu.VMEM((nhead // 2, seq, 2 * LANES), BF16),
            pltpu.VMEM((nhead, t, 2 * LANES), F32),
            pltpu.VMEM((nhead, t, LANES), F32),
            pltpu.VMEM((nhead, t, LANES), F32),
            pltpu.VMEM((nhead, t, LANES), F32),
            pltpu.SMEM((nhead, nblk), F32),
        ],
        compiler_params=pltpu.CompilerParams(dimension_semantics=("arbitrary",) * 3, vmem_limit_bytes=VMEM_LIMIT),
        name="fox_attention",
    )(bound, q, kt, v, c)


def _out_proj_kernel(x_ref, o_ref, w_ref, y_ref):
    y_ref[...] = x_ref[...] + jnp.dot(o_ref[...], w_ref[...], preferred_element_type=F32)


def _out_proj(x2, o, w_o):
    n, d = x2.shape
    tm = min(2 * TOKEN_TILE, n)
    return pl.pallas_call(
        _out_proj_kernel,
        out_shape=jax.ShapeDtypeStruct((n, d), F32),
        grid=(n // tm,),
        in_specs=[
            pl.BlockSpec((tm, d), lambda i: (i, 0)),
            pl.BlockSpec((tm, d), lambda i: (i, 0)),
            _const_spec((d, d)),
        ],
        out_specs=pl.BlockSpec((tm, d), lambda i: (i, 0)),
        compiler_params=_params(),
        name="fox_out_proj",
    )(x2, o, w_o.astype(BF16))


def _fox(x2, bsz, seq, g, w_in, b_f, q_gain, k_gain, w_o):
    q, kt, v, c = _fox_proj(x2, seq, g, w_in, b_f, q_gain, k_gain)
    bound = (QK_BOUND_SLACK * HEAD_DIM * LOG2E / math.sqrt(HEAD_DIM)) * jnp.max(jnp.abs(q_gain)) * jnp.max(
        jnp.abs(k_gain))
    o = _attention(bound.reshape(1).astype(F32), q, kt, v, c, bsz, seq)
    return _out_proj(x2, o, w_o)


def kernel(x, norm_mix, norm_ffn, conv_w_in, conv_b_in, conv_dw, conv_dw_b, conv_ln_g, conv_ln_b, conv_w_out,
           conv_b_out, pool_w, pool_b, pool_scale, fox_w_in, fox_b_f, fox_q_gain, fox_k_gain, fox_w_o, ffn_w_up,
           ffn_dw, ffn_dw_b, ffn_w_down):
    bsz, seq, d = x.shape
    depth = norm_mix.shape[0]
    x2 = x.reshape(bsz * seq, d)
    for i in range(depth):
        j = i // 3
        kind = i % 3
        if kind == 0:
            x2 = _conformer(x2, seq, norm_mix[i], conv_w_in[j], conv_b_in[j], conv_dw[j], conv_dw_b[j],
                            conv_ln_g[j], conv_ln_b[j], conv_w_out[j], conv_b_out[j])
        elif kind == 1:
            x2 = _pool(x2, seq, norm_mix[i], pool_w[j], pool_b[j], pool_scale[j])
        else:
            x2 = _fox(x2, bsz, seq, norm_mix[i], fox_w_in[j], fox_b_f[j], fox_q_gain[j], fox_k_gain[j], fox_w_o[j])
        x2 = _ffn(x2, seq, norm_ffn[i], ffn_w_up[i], ffn_dw[i], ffn_dw_b[i], ffn_w_down[i])
    return x2.reshape(bsz, seq, d)
```
